```python
import jax, jax.numpy as jnp
from jax import lax
import numpy as np

D_MODEL = 4096
BATCH = 4
SEQ = 2048
DEPTH = 2
DEC_BATCH = 8
DEC_SEQ = 1
PAST_LEN = 16384
PAGE_SIZE = 128

N_MIXERS = 2
N_LRU = (DEPTH + 1) // 2
N_FOX = DEPTH // 2
D_FF = 2 * D_MODEL
D_RNN = D_MODEL
LRU_BLOCK = 256
N_LRU_BLOCKS = D_RNN // LRU_BLOCK
CONV_W = 4
LRU_C = 8.0
HEAD_DIM = 128
N_HEADS = D_MODEL // HEAD_DIM
FOX_WIDTH = N_HEADS * HEAD_DIM
Q_BLOCK = 128
FORGET_BIAS_LO = 1.0
FORGET_BIAS_HI = 5.0
EPS = 1e-6

kernel_name = 'hybrid_rglru_fox_macaron_step'


def rms_norm(x, g):
    xf = x.astype(jnp.float32)
    y = xf * lax.rsqrt(jnp.mean(xf * xf, axis=-1, keepdims=True) + EPS)
    return (y * g.astype(jnp.float32)).astype(x.dtype)


def swiglu(x, w_gu, w_down):
    gu = x @ w_gu
    g, u = gu[..., :D_FF], gu[..., D_FF:]
    return (jax.nn.silu(g) * u) @ w_down


def _affine_combine(e1, e2):
    a1, b1 = e1
    a2, b2 = e2
    return a1 * a2, a2 * b1 + b2


def rglru_mixer(x, conv_buf, h0, w_in, conv_w, conv_b, w_a, b_a, w_i, b_i, lam, w_out):
    bsz, t = x.shape[0], x.shape[1]
    proj = x @ w_in
    y_br, xr = proj[..., :D_RNN], proj[..., D_RNN:]
    xpad = jnp.concatenate([conv_buf.astype(xr.dtype), xr], axis=1)
    xc = conv_b + conv_w[0] * xpad[:, 0:t]
    for k in range(1, CONV_W):
        xc = xc + conv_w[k] * xpad[:, k:k + t]
    new_buf = xpad[:, t:]
    xb = xc.reshape(bsz, t, N_LRU_BLOCKS, LRU_BLOCK)
    gate_r = jnp.einsum('bthi,hij->bthj', xb, w_a).reshape(bsz, t, D_RNN) + b_a
    gate_i = jnp.einsum('bthi,hij->bthj', xb, w_i).reshape(bsz, t, D_RNN) + b_i
    r = jax.nn.sigmoid(gate_r.astype(jnp.float32))
    ig = jax.nn.sigmoid(gate_i.astype(jnp.float32))
    log_a = -LRU_C * r * jax.nn.softplus(-lam.astype(jnp.float32))
    a = jnp.exp(log_a)
    b = jnp.sqrt(-jnp.expm1(2.0 * log_a)) * (ig * xc.astype(jnp.float32))
    a_cum, b_cum = lax.associative_scan(_affine_combine, (a, b), axis=1)
    h = a_cum * h0.astype(jnp.float32)[:, None, :] + b_cum
    out = (jax.nn.gelu(y_br.astype(jnp.float32)) * h).astype(x.dtype) @ w_out
    return out, new_buf, h[:, -1].astype(x.dtype)


def fox_project(x, w_in, b_f, q_g, k_g):
    bsz, t = x.shape[0], x.shape[1]
    proj = x @ w_in
    q = proj[..., 0:FOX_WIDTH].reshape(bsz, t, N_HEADS, HEAD_DIM)
    k = proj[..., FOX_WIDTH:2 * FOX_WIDTH].reshape(bsz, t, N_HEADS, HEAD_DIM)
    v = proj[..., 2 * FOX_WIDTH:3 * FOX_WIDTH].reshape(bsz, t, N_HEADS, HEAD_DIM)
    g = proj[..., 3 * FOX_WIDTH:4 * FOX_WIDTH].reshape(bsz, t, N_HEADS, HEAD_DIM)
    f_logit = proj[..., 4 * FOX_WIDTH:]
    q = rms_norm(q, q_g)
    k = rms_norm(k, k_g)
    logf = jax.nn.log_sigmoid(f_logit.astype(jnp.float32) + b_f.astype(jnp.float32))
    return q, k, v, g, logf


def fox_output(o, g, w_out):
    bsz, t = o.shape[0], o.shape[1]
    gated = o * jax.nn.sigmoid(g.astype(jnp.float32)).astype(o.dtype)
    return gated.reshape(bsz, t, FOX_WIDTH) @ w_out


def fox_attend_prompt(q, k, v, logf):
    bsz, t = q.shape[0], q.shape[1]
    c = jnp.cumsum(logf, axis=1).transpose(0, 2, 1)
    kpos = jnp.arange(t)
    scale = HEAD_DIM ** -0.5

    def block(start):
        qb = lax.dynamic_slice_in_dim(q, start, Q_BLOCK, axis=1)
        cb = lax.dynamic_slice_in_dim(c, start, Q_BLOCK, axis=2)
        s = jnp.einsum('bqhd,bkhd->bhqk', qb, k, preferred_element_type=jnp.float32) * scale
        s = s + cb[..., :, None] - c[..., None, :]
        qpos = start + jnp.arange(Q_BLOCK)
        s = jnp.where(kpos[None, :] <= qpos[:, None], s, -jnp.inf)
        p = jax.nn.softmax(s, axis=-1)
        return jnp.einsum('bhqk,bkhd->bqhd', p.astype(v.dtype), v)

    starts = jnp.arange(t // Q_BLOCK) * Q_BLOCK
    o = lax.map(block, starts)
    return o.transpose(1, 0, 2, 3, 4).reshape(bsz, t, N_HEADS, HEAD_DIM)


def fox_attend_sample(q, k, v, logf, cache_k, cache_v, cache_logf, layer, page_table):
    dbs, ds = q.shape[0], q.shape[1]
    n_pages = page_table.shape[1]
    past = n_pages * PAGE_SIZE
    scale = HEAD_DIM ** -0.5
    logf_past = cache_logf[layer, page_table].reshape(dbs, past, N_HEADS).astype(jnp.float32)
    c_all = jnp.cumsum(jnp.concatenate([logf_past, logf], axis=1), axis=1)
    c_q = c_all[:, past:].transpose(0, 2, 1)
    c_past = c_all[:, :past].reshape(dbs, n_pages, PAGE_SIZE, N_HEADS).transpose(1, 0, 3, 2)

    def page_step(carry, xs):
        m, l, acc = carry
        pt_col, c_pg = xs
        kp = cache_k[layer, pt_col].astype(q.dtype)
        vp = cache_v[layer, pt_col].astype(jnp.float32)
        s = jnp.einsum('bqhd,bkhd->bhqk', q, kp, preferred_element_type=jnp.float32) * scale
        s = s + c_q[..., None] - c_pg[:, :, None, :]
        m_new = jnp.maximum(m, jnp.max(s, axis=-1))
        alpha = jnp.exp(m - m_new)
        p = jnp.exp(s - m_new[..., None])
        l = l * alpha + jnp.sum(p, axis=-1)
        acc = acc * alpha[..., None] + jnp.einsum('bhqk,bkhd->bhqd', p, vp)
        return (m_new, l, acc), None

    init = (jnp.full((dbs, N_HEADS, ds), -jnp.inf, jnp.float32),
            jnp.zeros((dbs, N_HEADS, ds), jnp.float32),
            jnp.zeros((dbs, N_HEADS, ds, HEAD_DIM), jnp.float32))
    (m, l, acc), _ = lax.scan(page_step, init, (page_table.T, c_past))
    s = jnp.einsum('bqhd,bkhd->bhqk', q, k, preferred_element_type=jnp.float32) * scale
    s = s + c_q[..., :, None] - c_q[..., None, :]
    causal = jnp.arange(ds)[None, :] <= jnp.arange(ds)[:, None]
    s = jnp.where(causal, s, -jnp.inf)
    m_new = jnp.maximum(m, jnp.max(s, axis=-1))
    alpha = jnp.exp(m - m_new)
    p = jnp.exp(s - m_new[..., None])
    l = l * alpha + jnp.sum(p, axis=-1)
    acc = acc * alpha[..., None] + jnp.einsum('bhqk,bkhd->bhqd', p, v.astype(jnp.float32))
    o = (acc / l[..., None]).transpose(0, 2, 1, 3)
    return o.astype(q.dtype)


def setup_inputs(seed: int = 0) -> dict:
    key = jax.random.key(seed)
    ks = iter(jax.random.split(key, 32))
    f32 = jnp.float32
    n_pages = PAST_LEN // PAGE_SIZE
    n_used = DEC_BATCH * n_pages
    n_phys = n_used + (n_used + 3) // 4

    def nrm(shape, scale=1.0):
        return scale * jax.random.normal(next(ks), shape, f32)

    x_prompt = nrm((BATCH, SEQ, D_MODEL))
    x_sample = nrm((DEC_BATCH, DEC_SEQ, D_MODEL))
    state_conv = nrm((N_LRU, DEC_BATCH, CONV_W - 1, D_RNN))
    state_h = nrm((N_LRU, DEC_BATCH, D_RNN), 0.5)
    cache_k = nrm((N_FOX, n_phys, PAGE_SIZE, N_HEADS, HEAD_DIM))
    cache_v = nrm((N_FOX, n_phys, PAGE_SIZE, N_HEADS, HEAD_DIM))
    lf_shape = (N_FOX, n_phys, PAGE_SIZE, N_HEADS)
    cache_logf = jax.nn.log_sigmoid(
        jax.random.uniform(next(ks), lf_shape, f32, FORGET_BIAS_LO, FORGET_BIAS_HI) + nrm(lf_shape, 0.5))
    perm = jax.random.permutation(next(ks), n_phys)[:n_used]
    page_table = perm.reshape(DEC_BATCH, n_pages).astype(jnp.int32)

    norm_g = 1.0 + nrm((DEPTH, 3, D_MODEL), 0.05)
    w_ffn_gu = nrm((DEPTH, 2, D_MODEL, 2 * D_FF), D_MODEL ** -0.5)
    w_ffn_down = nrm((DEPTH, 2, D_FF, D_MODEL), D_FF ** -0.5)

    w_rec_in = nrm((N_LRU, D_MODEL, 2 * D_RNN), D_MODEL ** -0.5)
    conv_w = nrm((N_LRU, CONV_W, D_RNN), CONV_W ** -0.5)
    conv_b = nrm((N_LRU, D_RNN), 0.02)
    w_gate_a = nrm((N_LRU, N_LRU_BLOCKS, LRU_BLOCK, LRU_BLOCK), LRU_BLOCK ** -0.5)
    b_gate_a = nrm((N_LRU, D_RNN), 0.02)
    w_gate_i = nrm((N_LRU, N_LRU_BLOCKS, LRU_BLOCK, LRU_BLOCK), LRU_BLOCK ** -0.5)
    b_gate_i = nrm((N_LRU, D_RNN), 0.02)
    a0 = jax.random.uniform(next(ks), (N_LRU, D_RNN), f32, 0.9, 0.999)
    a_base = a0 ** (1.0 / LRU_C)
    lru_lambda = jnp.log(a_base) - jnp.log1p(-a_base)
    w_rec_out = nrm((N_LRU, D_RNN, D_MODEL), D_RNN ** -0.5)

    w_fox_in = nrm((N_FOX, D_MODEL, 4 * FOX_WIDTH + N_HEADS), D_MODEL ** -0.5)
    b_forget = jax.random.uniform(next(ks), (N_FOX, N_HEADS), f32, FORGET_BIAS_LO, FORGET_BIAS_HI)
    q_norm_g = 1.0 + nrm((N_FOX, HEAD_DIM), 0.05)
    k_norm_g = 1.0 + nrm((N_FOX, HEAD_DIM), 0.05)
    w_fox_out = nrm((N_FOX, FOX_WIDTH, D_MODEL), FOX_WIDTH ** -0.5)

    return {'x_prompt': x_prompt, 'x_sample': x_sample,
            'state_conv': state_conv, 'state_h': state_h,
            'cache_k': cache_k, 'cache_v': cache_v, 'cache_logf': cache_logf,
            'page_table': page_table,
            'norm_g': norm_g, 'w_ffn_gu': w_ffn_gu, 'w_ffn_down': w_ffn_down,
            'w_rec_in': w_rec_in, 'conv_w': conv_w, 'conv_b': conv_b,
            'w_gate_a': w_gate_a, 'b_gate_a': b_gate_a, 'w_gate_i': w_gate_i, 'b_gate_i': b_gate_i,
            'lru_lambda': lru_lambda, 'w_rec_out': w_rec_out,
            'w_fox_in': w_fox_in, 'b_forget': b_forget, 'q_norm_g': q_norm_g, 'k_norm_g': k_norm_g,
            'w_fox_out': w_fox_out}


def reference(x_prompt, x_sample, state_conv, state_h, cache_k, cache_v, cache_logf, page_table,
              norm_g, w_ffn_gu, w_ffn_down, w_rec_in, conv_w, conv_b, w_gate_a, b_gate_a,
              w_gate_i, b_gate_i, lru_lambda, w_rec_out, w_fox_in, b_forget, q_norm_g, k_norm_g,
              w_fox_out):
    xp, xs = x_prompt, x_sample
    bp = xp.shape[0]
    conv_p, h_p, k_p, v_p, lf_p = [], [], [], [], []
    conv_s, h_s, k_s, v_s, lf_s = [], [], [], [], []
    for i in range(DEPTH):
        j = i // N_MIXERS
        xp = xp + 0.5 * swiglu(rms_norm(xp, norm_g[i, 0]), w_ffn_gu[i, 0], w_ffn_down[i, 0])
        xs = xs + 0.5 * swiglu(rms_norm(xs, norm_g[i, 0]), w_ffn_gu[i, 0], w_ffn_down[i, 0])
        up = rms_norm(xp, norm_g[i, 1])
        us = rms_norm(xs, norm_g[i, 1])
        if i % N_MIXERS == 0:
            lru_w = (w_rec_in[j], conv_w[j], conv_b[j], w_gate_a[j], b_gate_a[j],
                     w_gate_i[j], b_gate_i[j], lru_lambda[j], w_rec_out[j])
            zero_buf = jnp.zeros((bp, CONV_W - 1, D_RNN), xp.dtype)
            zero_h = jnp.zeros((bp, D_RNN), xp.dtype)
            mp, cbp, hlp = rglru_mixer(up, zero_buf, zero_h, *lru_w)
            ms, cbs, hls = rglru_mixer(us, state_conv[j], state_h[j], *lru_w)
            conv_p.append(cbp)
            h_p.append(hlp)
            conv_s.append(cbs)
            h_s.append(hls)
        else:
            qp, kp, vp, gp, lfp = fox_project(up, w_fox_in[j], b_forget[j], q_norm_g[j], k_norm_g[j])
            mp = fox_output(fox_attend_prompt(qp, kp, vp, lfp), gp, w_fox_out[j])
            qs, ks_, vs_, gs, lfs = fox_project(us, w_fox_in[j], b_forget[j], q_norm_g[j], k_norm_g[j])
            os_ = fox_attend_sample(qs, ks_, vs_, lfs, cache_k, cache_v, cache_logf, j, page_table)
            ms = fox_output(os_, gs, w_fox_out[j])
            k_p.append(kp)
            v_p.append(vp)
            lf_p.append(lfp)
            k_s.append(ks_)
            v_s.append(vs_)
            lf_s.append(lfs)
        xp = xp + mp
        xs = xs + ms
        xp = xp + 0.5 * swiglu(rms_norm(xp, norm_g[i, 2]), w_ffn_gu[i, 1], w_ffn_down[i, 1])
        xs = xs + 0.5 * swiglu(rms_norm(xs, norm_g[i, 2]), w_ffn_gu[i, 1], w_ffn_down[i, 1])
    return (xp, xs, jnp.stack(conv_p), jnp.stack(h_p), jnp.stack(k_p), jnp.stack(v_p), jnp.stack(lf_p),
            jnp.stack(conv_s), jnp.stack(h_s), jnp.stack(k_s), jnp.stack(v_s), jnp.stack(lf_s))
```

```python
import functools

import jax
import jax.numpy as jnp
from jax import lax
from jax.experimental import pallas as pl
from jax.experimental.pallas import tpu as pltpu

F32 = jnp.float32
BF16 = jnp.bfloat16

EPS = 1e-6
LRU_C = 8.0
LRU_BLOCK = 256
CONV_W = 4
HEAD_DIM = 128
PAGE_SIZE = 128

LANE = 128
SUBLANE = 8
BF16_ROWS = 16
VMEM_BYTES_V7X = 64 * 1024 * 1024
VMEM_LIMIT = VMEM_BYTES_V7X - 8 * 1024 * 1024

_NT_DIMS = (((1,), (1,)), ((), ()))


def _params(*semantics):
    return pltpu.CompilerParams(dimension_semantics=semantics, vmem_limit_bytes=VMEM_LIMIT)


def _rms_body(x_ref, g_ref, o_ref):
    x = x_ref[...]
    ms = jnp.mean(x * x, axis=-1, keepdims=True)
    o_ref[...] = (x * lax.rsqrt(ms + EPS) * g_ref[...]).astype(o_ref.dtype)


def _rmsnorm(x, g):
    m, d = x.shape
    tr = min(m, 256)
    return pl.pallas_call(
        _rms_body,
        grid=(m // tr,),
        in_specs=[pl.BlockSpec((tr, d), lambda i: (i, 0)),
                  pl.BlockSpec((1, d), lambda i: (0, 0))],
        out_specs=pl.BlockSpec((tr, d), lambda i: (i, 0)),
        out_shape=jax.ShapeDtypeStruct((m, d), BF16),
        compiler_params=_params("parallel"),
        name="rmsnorm",
    )(x, g.reshape(1, d))


def _swiglu_body(a_ref, wg_ref, wu_ref, o_ref):
    a = a_ref[...]
    g = jnp.dot(a, wg_ref[...], preferred_element_type=F32)
    u = jnp.dot(a, wu_ref[...], preferred_element_type=F32)
    o_ref[...] = (jax.nn.silu(g) * u).astype(o_ref.dtype)


def _ffn_up(n, w_gu, tm, tn):
    m, d = n.shape
    dff = w_gu.shape[1] // 2
    nj = dff // tn
    return pl.pallas_call(
        _swiglu_body,
        grid=(m // tm, nj),
        in_specs=[pl.BlockSpec((tm, d), lambda i, j: (i, 0)),
                  pl.BlockSpec((d, tn), lambda i, j: (0, j)),
                  pl.BlockSpec((d, tn), lambda i, j: (0, j + nj))],
        out_specs=pl.BlockSpec((tm, tn), lambda i, j: (i, j)),
        out_shape=jax.ShapeDtypeStruct((m, dff), BF16),
        compiler_params=_params("parallel", "parallel"),
        name="ffn_up",
    )(n, w_gu, w_gu)


def _mm_res_body(a_ref, w_ref, r_ref, o_ref, *scratch, nk, scale):
    part = jnp.dot(a_ref[...], w_ref[...], preferred_element_type=F32)
    if nk == 1:
        o_ref[...] = r_ref[...] + scale * part
        return
    acc_ref, = scratch
    k = pl.program_id(2)

    @pl.when(k == 0)
    def _():
        acc_ref[...] = part

    @pl.when(jnp.logical_and(k > 0, k < nk - 1))
    def _():
        acc_ref[...] += part

    @pl.when(k == nk - 1)
    def _():
        o_ref[...] = r_ref[...] + scale * (acc_ref[...] + part)


def _mm_residual(a, w, res, scale, tm, tn, tk):
    m, kdim = a.shape
    n = w.shape[1]
    nk = kdim // tk
    scratch = [pltpu.VMEM((tm, tn), F32)] if nk > 1 else []
    return pl.pallas_call(
        functools.partial(_mm_res_body, nk=nk, scale=scale),
        grid=(m // tm, n // tn, nk),
        in_specs=[pl.BlockSpec((tm, tk), lambda i, j, k: (i, k)),
                  pl.BlockSpec((tk, tn), lambda i, j, k: (k, j)),
                  pl.BlockSpec((tm, tn), lambda i, j, k: (i, j))],
        out_specs=pl.BlockSpec((tm, tn), lambda i, j, k: (i, j)),
        out_shape=jax.ShapeDtypeStruct((m, n), F32),
        scratch_shapes=scratch,
        compiler_params=_params("parallel", "parallel", "arbitrary"),
        name="mm_residual",
    )(a, w, res)


def _proj_body(a_ref, w_ref, *rest, tn, head_norm):
    acc = jnp.dot(a_ref[...], w_ref[...], preferred_element_type=F32)
    if not head_norm:
        o_ref, = rest
        o_ref[...] = acc.astype(o_ref.dtype)
        return
    g_ref, o_ref = rest
    g = g_ref[...]
    for hh in range(tn // HEAD_DIM):
        sl = slice(hh * HEAD_DIM, (hh + 1) * HEAD_DIM)
        blk = acc[:, sl]
        ms = jnp.mean(blk * blk, axis=-1, keepdims=True)
        o_ref[:, sl] = (blk * lax.rsqrt(ms + EPS) * g).astype(o_ref.dtype)


def _proj(a, w, col0, ncols, tm, tn, out_dtype, head_gain=None):
    m, kdim = a.shape
    joff = col0 // tn
    in_specs = [pl.BlockSpec((tm, kdim), lambda i, j: (i, 0)),
                pl.BlockSpec((kdim, tn), lambda i, j: (0, j + joff))]
    args = [a, w]
    if head_gain is not None:
        in_specs.append(pl.BlockSpec((1, HEAD_DIM), lambda i, j: (0, 0)))
        args.append(head_gain.reshape(1, HEAD_DIM))
    return pl.pallas_call(
        functools.partial(_proj_body, tn=tn, head_norm=head_gain is not None),
        grid=(m // tm, ncols // tn),
        in_specs=in_specs,
        out_specs=pl.BlockSpec((tm, tn), lambda i, j: (i, j)),
        out_shape=jax.ShapeDtypeStruct((m, ncols), out_dtype),
        compiler_params=_params("parallel", "parallel"),
        name="proj",
    )(*args)


def _lru_gates(xc, wa_ref, wi_ref, ba, bi, lam, nblk):
    xcb = xc.astype(BF16)
    a_parts, b_parts = [], []
    for blk in range(nblk):
        sl = slice(blk * LRU_BLOCK, (blk + 1) * LRU_BLOCK)
        gate_r = jnp.dot(xcb[:, sl], wa_ref[blk], preferred_element_type=F32) + ba[:, sl]
        gate_i = jnp.dot(xcb[:, sl], wi_ref[blk], preferred_element_type=F32) + bi[:, sl]
        r = jax.nn.sigmoid(gate_r)
        ig = jax.nn.sigmoid(gate_i)
        log_a = -LRU_C * r * jax.nn.softplus(-lam[:, sl])
        a_parts.append(jnp.exp(log_a))
        th = jnp.tanh(log_a)
        b_parts.append(jnp.sqrt(-2.0 * th / (1.0 - th)) * (ig * xc[:, sl]))
    return a_parts, b_parts


def _lru_scan_body(xr_ref, y_ref, wa_ref, wi_ref, ba_ref, bi_ref, lam_ref, cw_ref, cb_ref,
                   o_ref, hl_ref, cbuf_ref, xpad, a_scr, b_scr, h_scr, *, tt, cw_cols):
    t = pl.program_id(2)
    nblk = cw_cols // LRU_BLOCK

    @pl.when(t == 0)
    def _():
        xpad[0:SUBLANE, :] = jnp.zeros((SUBLANE, cw_cols), F32)
        h_scr[...] = jnp.zeros((SUBLANE, cw_cols), F32)

    xr = xr_ref[...]
    xpad[SUBLANE:SUBLANE + tt, :] = xr
    cw = cw_ref[...]
    xc = cb_ref[...] + cw[0:1] * xpad[SUBLANE - 3:SUBLANE - 3 + tt, :]
    xc = xc + cw[1:2] * xpad[SUBLANE - 2:SUBLANE - 2 + tt, :]
    xc = xc + cw[2:3] * xpad[SUBLANE - 1:SUBLANE - 1 + tt, :]
    xc = xc + cw[3:4] * xr
    a_parts, b_parts = _lru_gates(xc, wa_ref, wi_ref, ba_ref[...], bi_ref[...], lam_ref[...], nblk)
    for blk in range(nblk):
        sl = slice(blk * LRU_BLOCK, (blk + 1) * LRU_BLOCK)
        a_scr[:, sl] = a_parts[blk]
        b_scr[:, sl] = b_parts[blk]

    row = lax.broadcasted_iota(jnp.int32, (SUBLANE, cw_cols), 0)

    def group(gi, h):
        off = pl.multiple_of(gi * SUBLANE, SUBLANE)
        a = a_scr[pl.ds(off, SUBLANE), :]
        b = b_scr[pl.ds(off, SUBLANE), :]
        for s in (1, 2, 4):
            a_sh = jnp.where(row >= s, pltpu.roll(a, s, 0), 1.0)
            b_sh = jnp.where(row >= s, pltpu.roll(b, s, 0), 0.0)
            b = a * b_sh + b
            a = a * a_sh
        hcur = a * h + b
        a_scr[pl.ds(off, SUBLANE), :] = hcur
        return jnp.broadcast_to(hcur[SUBLANE - 1:SUBLANE, :], (SUBLANE, cw_cols))

    h = lax.fori_loop(0, tt // SUBLANE, group, h_scr[...], unroll=4)
    h_scr[...] = h
    xpad[0:SUBLANE, :] = xpad[tt:tt + SUBLANE, :]
    o_ref[...] = (jax.nn.gelu(y_ref[...]) * a_scr[...]).astype(o_ref.dtype)

    @pl.when(t == pl.num_programs(2) - 1)
    def _():
        hl_ref[0] = h[0:1, :]
        cbuf_ref[0] = xr_ref[tt - (CONV_W - 1):tt, :]


def _lru_prompt(proj, bsz, seq, wa, wi, ba, bi, lam, cw, cb, tt=512, cw_cols=1024):
    d = proj.shape[1] // 2
    nc = d // cw_cols
    nt = seq // tt
    gpb = cw_cols // LRU_BLOCK
    vec = lambda: pl.BlockSpec((1, cw_cols), lambda b, c, t: (0, c))
    return pl.pallas_call(
        functools.partial(_lru_scan_body, tt=tt, cw_cols=cw_cols),
        grid=(bsz, nc, nt),
        in_specs=[pl.BlockSpec((tt, cw_cols), lambda b, c, t: (b * nt + t, nc + c)),
                  pl.BlockSpec((tt, cw_cols), lambda b, c, t: (b * nt + t, c)),
                  pl.BlockSpec((gpb, LRU_BLOCK, LRU_BLOCK), lambda b, c, t: (c, 0, 0)),
                  pl.BlockSpec((gpb, LRU_BLOCK, LRU_BLOCK), lambda b, c, t: (c, 0, 0)),
                  vec(), vec(), vec(),
                  pl.BlockSpec((CONV_W, cw_cols), lambda b, c, t: (0, c)),
                  vec()],
        out_specs=[pl.BlockSpec((tt, cw_cols), lambda b, c, t: (b * nt + t, c)),
                   pl.BlockSpec((1, 1, cw_cols), lambda b, c, t: (b, 0, c)),
                   pl.BlockSpec((1, CONV_W - 1, cw_cols), lambda b, c, t: (b, 0, c))],
        out_shape=[jax.ShapeDtypeStruct((bsz * seq, d), BF16),
                   jax.ShapeDtypeStruct((bsz, 1, d), F32),
                   jax.ShapeDtypeStruct((bsz, CONV_W - 1, d), F32)],
        scratch_shapes=[pltpu.VMEM((tt + SUBLANE, cw_cols), F32),
                        pltpu.VMEM((tt, cw_cols), F32),
                        pltpu.VMEM((tt, cw_cols), F32),
                        pltpu.VMEM((SUBLANE, cw_cols), F32)],
        compiler_params=_params("parallel", "parallel", "arbitrary"),
        name="lru_scan",
    )(proj, proj, wa, wi, ba.reshape(1, d), bi.reshape(1, d), lam.reshape(1, d), cw, cb.reshape(1, d))


def _lru_step_body(xr_ref, y_ref, b0_ref, b1_ref, b2_ref, h0_ref, wa_ref, wi_ref, ba_ref, bi_ref,
                   lam_ref, cw_ref, cb_ref, o_ref, h_ref, *, cw_cols):
    nblk = cw_cols // LRU_BLOCK
    cw = cw_ref[...]
    xr = xr_ref[...]
    xc = cb_ref[...] + cw[0:1] * b0_ref[...]
    xc = xc + cw[1:2] * b1_ref[...]
    xc = xc + cw[2:3] * b2_ref[...]
    xc = xc + cw[3:4] * xr
    a_parts, b_parts = _lru_gates(xc, wa_ref, wi_ref, ba_ref[...], bi_ref[...], lam_ref[...], nblk)
    y = y_ref[...]
    for blk in range(nblk):
        sl = slice(blk * LRU_BLOCK, (blk + 1) * LRU_BLOCK)
        h = a_parts[blk] * h0_ref[:, sl] + b_parts[blk]
        h_ref[:, sl] = h
        o_ref[:, sl] = (jax.nn.gelu(y[:, sl]) * h).astype(o_ref.dtype)


def _lru_step(proj, bufs, h0, wa, wi, ba, bi, lam, cw, cb, cw_cols=1024):
    rows = proj.shape[0]
    d = proj.shape[1] // 2
    nc = d // cw_cols
    gpb = cw_cols // LRU_BLOCK
    act = lambda off: pl.BlockSpec((rows, cw_cols), lambda c: (0, c + off))
    vec = lambda: pl.BlockSpec((1, cw_cols), lambda c: (0, c))
    return pl.pallas_call(
        functools.partial(_lru_step_body, cw_cols=cw_cols),
        grid=(nc,),
        in_specs=[act(nc), act(0), act(0), act(0), act(0), act(0),
                  pl.BlockSpec((gpb, LRU_BLOCK, LRU_BLOCK), lambda c: (c, 0, 0)),
                  pl.BlockSpec((gpb, LRU_BLOCK, LRU_BLOCK), lambda c: (c, 0, 0)),
                  vec(), vec(), vec(),
                  pl.BlockSpec((CONV_W, cw_cols), lambda c: (0, c)),
                  vec()],
        out_specs=[act(0), act(0)],
        out_shape=[jax.ShapeDtypeStruct((rows, d), BF16),
                   jax.ShapeDtypeStruct((rows, d), F32)],
        compiler_params=_params("parallel"),
        name="lru_step",
    )(proj, proj, bufs[0], bufs[1], bufs[2], h0, wa, wi,
      ba.reshape(1, d), bi.reshape(1, d), lam.reshape(1, d), cw, cb.reshape(1, d))


def _split3(x):
    hi = x.astype(BF16)
    r1 = x - hi.astype(F32)
    mid = r1.astype(BF16)
    lo = (r1 - mid.astype(F32)).astype(BF16)
    return hi, mid, lo


def _dot_x_ones(x, ones_rhs):
    hi, mid, lo = _split3(x)
    d = lambda p: jnp.dot(p, ones_rhs, preferred_element_type=F32)
    return d(hi) + d(mid) + d(lo)


def _dot_ones_x(ones_lhs, x):
    hi, mid, lo = _split3(x)
    d = lambda p: jnp.dot(ones_lhs, p, preferred_element_type=F32)
    return d(hi) + d(mid) + d(lo)


def _logf_body(wft_ref, n_ref, bf_ref, u_ref, lf_ref, c_ref, carry, *, tt):
    t = pl.program_id(1)

    @pl.when(t == 0)
    def _():
        carry[...] = jnp.zeros(carry.shape, F32)

    f = lax.dot_general(wft_ref[...], n_ref[...], _NT_DIMS, preferred_element_type=F32)
    lf = jax.nn.log_sigmoid(f + bf_ref[...])
    lf_ref[...] = lf
    u = u_ref[...]
    c = carry[...]
    nh = lf.shape[0]
    for kk in range(tt // LANE):
        sl = slice(kk * LANE, (kk + 1) * LANE)
        cs = _dot_x_ones(lf[:, sl], u) + c
        c_ref[:, sl] = cs
        c = jnp.broadcast_to(cs[:, LANE - 1:LANE], (nh, LANE))
    carry[...] = c


def _logf_prompt(n, wft, b_f, bsz, seq, tt=512):
    m, d = n.shape
    nh = wft.shape[0]
    nt = seq // tt
    upper = (lax.broadcasted_iota(jnp.int32, (LANE, LANE), 0)
             <= lax.broadcasted_iota(jnp.int32, (LANE, LANE), 1)).astype(BF16)
    return pl.pallas_call(
        functools.partial(_logf_body, tt=tt),
        grid=(bsz, nt),
        in_specs=[pl.BlockSpec((nh, d), lambda b, t: (0, 0)),
                  pl.BlockSpec((tt, d), lambda b, t: (b * nt + t, 0)),
                  pl.BlockSpec((nh, 1), lambda b, t: (0, 0)),
                  pl.BlockSpec((LANE, LANE), lambda b, t: (0, 0))],
        out_specs=[pl.BlockSpec((nh, tt), lambda b, t: (0, b * nt + t)),
                   pl.BlockSpec((nh, tt), lambda b, t: (0, b * nt + t))],
        out_shape=[jax.ShapeDtypeStruct((nh, m), F32),
                   jax.ShapeDtypeStruct((nh, m), F32)],
        scratch_shapes=[pltpu.VMEM((nh, LANE), F32)],
        compiler_params=_params("parallel", "arbitrary"),
        name="logf_prompt",
    )(wft, n, b_f.reshape(nh, 1), upper)


def _flash_body(q_ref, k_ref, v_ref, g_ref, c_ref, o_ref, m_scr, l_scr, acc_scr, *, tq, scale):
    qi = pl.program_id(2)
    q = q_ref[...]
    m_scr[...] = jnp.full(m_scr.shape, -jnp.inf, F32)
    l_scr[...] = jnp.zeros(l_scr.shape, F32)
    acc_scr[...] = jnp.zeros(acc_scr.shape, F32)

    def block(kj, masked):
        off = pl.multiple_of(kj * tq, tq)
        k = k_ref[pl.ds(off, tq), :].astype(BF16)
        v = v_ref[pl.ds(off, tq), :].astype(BF16)
        s = lax.dot_general(q, k, _NT_DIMS, preferred_element_type=F32) * scale - c_ref[0, kj]
        if masked:
            row = lax.broadcasted_iota(jnp.int32, (tq, tq), 0)
            col = lax.broadcasted_iota(jnp.int32, (tq, tq), 1)
            s = jnp.where(col <= row, s, -jnp.inf)
        m_prev = m_scr[...]
        m_new = jnp.maximum(m_prev, jnp.max(s, axis=-1, keepdims=True))
        alpha = jnp.exp(m_prev - m_new)
        p = jnp.exp(s - m_new)
        l_scr[...] = alpha * l_scr[...] + jnp.sum(p, axis=-1, keepdims=True)
        acc_scr[...] = alpha * acc_scr[...] + jnp.dot(p.astype(BF16), v, preferred_element_type=F32)
        m_scr[...] = m_new

    def body(kj, carry):
        block(kj, False)
        return carry

    lax.fori_loop(0, qi, body, 0)
    block(qi, True)
    o = acc_scr[...] / l_scr[...]
    o_ref[...] = (o * jax.nn.sigmoid(g_ref[...])).astype(o_ref.dtype)


def _fox_attend_prompt(q, k, v, g, c_rows, bsz, seq, tq=512):
    m, width = q.shape
    nh = width // HEAD_DIM
    nq = seq // tq
    return pl.pallas_call(
        functools.partial(_flash_body, tq=tq, scale=HEAD_DIM ** -0.5),
        grid=(bsz, nh, nq),
        in_specs=[pl.BlockSpec((tq, HEAD_DIM), lambda b, h, i: (b * nq + i, h)),
                  pl.BlockSpec((seq, HEAD_DIM), lambda b, h, i: (b, h)),
                  pl.BlockSpec((seq, HEAD_DIM), lambda b, h, i: (b, h)),
                  pl.BlockSpec((tq, HEAD_DIM), lambda b, h, i: (b * nq + i, h)),
                  pl.BlockSpec((1, nq, 1, tq), lambda b, h, i: (b * nh + h, 0, 0, 0))],
        out_specs=pl.BlockSpec((tq, HEAD_DIM), lambda b, h, i: (b * nq + i, h)),
        out_shape=jax.ShapeDtypeStruct((m, width), BF16),
        scratch_shapes=[pltpu.VMEM((tq, 1), F32), pltpu.VMEM((tq, 1), F32),
                        pltpu.VMEM((tq, HEAD_DIM), F32)],
        compiler_params=_params("parallel", "parallel", "parallel"),
        name="fox_prompt_attn",
    )(q, k, v, g, c_rows)


def _page_cumsum_body(pt_ref, *refs, pages):
    lf_refs = refs[:pages]
    ltri_ref, o_ref, carry = refs[pages], refs[pages + 1], refs[pages + 2]
    j = pl.program_id(1)

    @pl.when(j == 0)
    def _():
        carry[...] = jnp.zeros(carry.shape, F32)

    ltri = ltri_ref[...]
    c = carry[...]
    for p in range(pages):
        cs = _dot_ones_x(ltri, lf_refs[p][0, 0]) + c
        o_ref[0, p] = cs
        c = cs[PAGE_SIZE - 1:PAGE_SIZE, :]
    carry[...] = c


def _page_cumsum(cache_logf, layer, page_table, pages=8):
    dbs, n_pages = page_table.shape
    nh = cache_logf.shape[-1]
    lower = (lax.broadcasted_iota(jnp.int32, (PAGE_SIZE, PAGE_SIZE), 1)
             <= lax.broadcasted_iota(jnp.int32, (PAGE_SIZE, PAGE_SIZE), 0)).astype(BF16)

    def page_spec(p):
        return pl.BlockSpec((1, 1, PAGE_SIZE, nh),
                            lambda b, j, pt: (layer, pt[b, j * pages + p], 0, 0))

    grid_spec = pltpu.PrefetchScalarGridSpec(
        num_scalar_prefetch=1,
        grid=(dbs, n_pages // pages),
        in_specs=[page_spec(p) for p in range(pages)]
        + [pl.BlockSpec((PAGE_SIZE, PAGE_SIZE), lambda b, j, pt: (0, 0))],
        out_specs=pl.BlockSpec((1, pages, PAGE_SIZE, nh), lambda b, j, pt: (b, j, 0, 0)),
        scratch_shapes=[pltpu.VMEM((1, nh), F32)])
    return pl.pallas_call(
        functools.partial(_page_cumsum_body, pages=pages),
        grid_spec=grid_spec,
        out_shape=jax.ShapeDtypeStruct((dbs, n_pages, PAGE_SIZE, nh), F32),
        compiler_params=_params("parallel", "arbitrary"),
        name="page_cumsum",
    )(page_table, *([cache_logf] * pages), lower)


def _decode_body(pt_ref, *refs, pages, scale):
    q_ref = refs[0]
    k_refs = refs[1:1 + pages]
    v_refs = refs[1 + pages:1 + 2 * pages]
    (c_ref, hb_ref, kn_ref, vn_ref, lfn_ref, ctot_ref, g_ref,
     o_ref, m_scr, l_scr, acc_scr) = refs[1 + 2 * pages:]
    j = pl.program_id(1)

    @pl.when(j == 0)
    def _():
        m_scr[...] = jnp.full(m_scr.shape, -jnp.inf, F32)
        l_scr[...] = jnp.zeros(l_scr.shape, F32)
        acc_scr[...] = jnp.zeros(acc_scr.shape, F32)

    q = q_ref[0]
    hb = hb_ref[...]
    nh = q.shape[0]
    for p in range(pages):
        k = k_refs[p][0, 0].reshape(PAGE_SIZE * nh, HEAD_DIM).astype(BF16)
        v = v_refs[p][0, 0].reshape(PAGE_SIZE * nh, HEAD_DIM).astype(BF16)
        s = lax.dot_general(q, k, _NT_DIMS, preferred_element_type=F32) * scale
        s = s - c_ref[0, p] + hb
        m_prev = m_scr[...]
        m_new = jnp.maximum(m_prev, jnp.max(s, axis=-1, keepdims=True))
        alpha = jnp.exp(m_prev - m_new)
        pr = jnp.exp(s - m_new)
        l_scr[...] = alpha * l_scr[...] + jnp.sum(pr, axis=-1, keepdims=True)
        acc_scr[...] = alpha * acc_scr[...] + jnp.dot(pr.astype(BF16), v, preferred_element_type=F32)
        m_scr[...] = m_new

    @pl.when(j == pl.num_programs(1) - 1)
    def _():
        kn = kn_ref[0].astype(BF16).astype(F32)
        s_self = jnp.sum(q.astype(F32) * kn, axis=-1, keepdims=True) * scale
        s_self = s_self - (ctot_ref[0] + lfn_ref[0])
        m_prev = m_scr[...]
        m_new = jnp.maximum(m_prev, s_self)
        alpha = jnp.exp(m_prev - m_new)
        p_self = jnp.exp(s_self - m_new)
        l = alpha * l_scr[...] + p_self
        acc = alpha * acc_scr[...] + p_self * vn_ref[0]
        o_ref[0] = (acc / l) * jax.nn.sigmoid(g_ref[0])


def _fox_attend_sample(q, k_new, v_new, g, lf_new, c_tot, c_rows, head_bias,
                       cache_k, cache_v, layer, page_table, pages=4):
    dbs, n_pages = page_table.shape
    nh = q.shape[1]
    row_w = PAGE_SIZE * nh

    def page_spec(p):
        return pl.BlockSpec((1, 1, PAGE_SIZE, nh, HEAD_DIM),
                            lambda b, j, pt: (layer, pt[b, j * pages + p], 0, 0, 0))

    tok = lambda: pl.BlockSpec((1, nh, HEAD_DIM), lambda b, j, pt: (b, 0, 0))
    col = lambda: pl.BlockSpec((1, nh, 1), lambda b, j, pt: (b, 0, 0))
    grid_spec = pltpu.PrefetchScalarGridSpec(
        num_scalar_prefetch=1,
        grid=(dbs, n_pages // pages),
        in_specs=[tok()]
        + [page_spec(p) for p in range(pages)]
        + [page_spec(p) for p in range(pages)]
        + [pl.BlockSpec((1, pages, 1, row_w), lambda b, j, pt: (b, j, 0, 0)),
           pl.BlockSpec((nh, row_w), lambda b, j, pt: (0, 0)),
           tok(), tok(), col(), col(), tok()],
        out_specs=tok(),
        scratch_shapes=[pltpu.VMEM((nh, 1), F32), pltpu.VMEM((nh, 1), F32),
                        pltpu.VMEM((nh, HEAD_DIM), F32)])
    return pl.pallas_call(
        functools.partial(_decode_body, pages=pages, scale=HEAD_DIM ** -0.5),
        grid_spec=grid_spec,
        out_shape=jax.ShapeDtypeStruct((dbs, nh, HEAD_DIM), F32),
        compiler_params=_params("parallel", "arbitrary"),
        name="fox_sample_attn",
    )(page_table, q, *([cache_k] * pages), *([cache_v] * pages),
      c_rows, head_bias, k_new, v_new, lf_new, c_tot, g)


def _logf_step_body(n_ref, wf_ref, bf_ref, o_ref):
    f = jnp.dot(n_ref[...], wf_ref[...], preferred_element_type=F32)
    o_ref[...] = jax.nn.log_sigmoid(f + bf_ref[...])


def _logf_step(n, wf, b_f):
    rows = n.shape[0]
    nh = wf.shape[1]
    return pl.pallas_call(
        _logf_step_body,
        out_shape=jax.ShapeDtypeStruct((rows, nh), F32),
        name="logf_step",
    )(n, wf, b_f.reshape(1, nh))


def _ffn_half(x, g, w_gu, w_down, tm, tn_up, tn_down, tk_down):
    n = _rmsnorm(x, g)
    hidden = _ffn_up(n, w_gu, tm, tn_up)
    return _mm_residual(hidden, w_down, x, 0.5, tm, tn_down, tk_down)


def kernel(x_prompt, x_sample, state_conv, state_h, cache_k, cache_v, cache_logf, page_table,
           norm_g, w_ffn_gu, w_ffn_down, w_rec_in, conv_w, conv_b, w_gate_a, b_gate_a,
           w_gate_i, b_gate_i, lru_lambda, w_rec_out, w_fox_in, b_forget, q_norm_g, k_norm_g,
           w_fox_out):
    bsz, seq, d = x_prompt.shape
    dbs, dseq, _ = x_sample.shape
    assert dseq == 1, "sample group decodes one token per sequence"
    depth = norm_g.shape[0]
    nh = b_forget.shape[1]
    width = nh * HEAD_DIM
    n_pages = page_table.shape[1]
    srows = BF16_ROWS
    assert dbs <= srows

    w_gu_b = w_ffn_gu.astype(BF16)
    w_down_b = w_ffn_down.astype(BF16)
    w_rec_in_b = w_rec_in.astype(BF16)
    w_rec_out_b = w_rec_out.astype(BF16)
    w_ga_b = w_gate_a.astype(BF16)
    w_gi_b = w_gate_i.astype(BF16)
    w_fox_in_b = w_fox_in.astype(BF16)
    w_fox_out_b = w_fox_out.astype(BF16)

    xp = x_prompt.reshape(bsz * seq, d)
    xs = jnp.pad(x_sample.reshape(dbs, d), ((0, srows - dbs), (0, 0)))

    ptile = dict(tm=1024, tn_up=512, tn_down=1024, tk_down=2048)
    stile = dict(tm=srows, tn_up=1024, tn_down=2048, tk_down=4096)

    conv_p, h_p, k_p, v_p, lf_p = [], [], [], [], []
    conv_s, h_s, k_s, v_s, lf_s = [], [], [], [], []
    for i in range(depth):
        j = i // 2
        xp = _ffn_half(xp, norm_g[i, 0], w_gu_b[i, 0], w_down_b[i, 0], **ptile)
        xs = _ffn_half(xs, norm_g[i, 0], w_gu_b[i, 0], w_down_b[i, 0], **stile)
        up = _rmsnorm(xp, norm_g[i, 1])
        us = _rmsnorm(xs, norm_g[i, 1])
        if i % 2 == 0:
            lru_w = (w_ga_b[j], w_gi_b[j], b_gate_a[j], b_gate_i[j], lru_lambda[j], conv_w[j], conv_b[j])
            proj_p = _proj(up, w_rec_in_b[j], 0, 2 * d, 1024, 1024, F32)
            gated_p, hl, cbuf = _lru_prompt(proj_p, bsz, seq, *lru_w)
            xp = _mm_residual(gated_p, w_rec_out_b[j], xp, 1.0, 1024, 1024, d)
            conv_p.append(cbuf)
            h_p.append(hl.reshape(bsz, d))

            proj_s = _proj(us, w_rec_in_b[j], 0, 2 * d, srows, 2048, F32)
            pad_rows = lambda a: jnp.pad(a, ((0, srows - dbs), (0, 0)))
            bufs = [pad_rows(state_conv[j, :, r, :]) for r in range(CONV_W - 1)]
            gated_s, h_new = _lru_step(proj_s, bufs, pad_rows(state_h[j]), *lru_w)
            xs = _mm_residual(gated_s, w_rec_out_b[j], xs, 1.0, srows, 2048, d)
            conv_s.append(jnp.stack([state_conv[j, :, 1, :], state_conv[j, :, 2, :],
                                     proj_s[:dbs, d:]], axis=1))
            h_s.append(h_new[:dbs])
        else:
            w_in = w_fox_in_b[j]
            w_f = w_in[:, 4 * width:]
            q = _proj(up, w_in, 0, width, 1024, 1024, BF16, q_norm_g[j])
            k = _proj(up, w_in, width, width, 1024, 1024, F32, k_norm_g[j])
            v = _proj(up, w_in, 2 * width, width, 1024, 1024, F32)
            g = _proj(up, w_in, 3 * width, width, 1024, 1024, F32)
            lf_t, c_t = _logf_prompt(up, w_f.T, b_forget[j], bsz, seq)
            tq = 512
            c_rows = c_t.reshape(nh, bsz, seq // tq, 1, tq).transpose(1, 0, 2, 3, 4)
            c_rows = c_rows.reshape(bsz * nh, seq // tq, 1, tq)
            o = _fox_attend_prompt(q, k, v, g, c_rows, bsz, seq, tq)
            xp = _mm_residual(o, w_fox_out_b[j], xp, 1.0, 1024, 1024, width)
            k_p.append(k.reshape(bsz, seq, nh, HEAD_DIM))
            v_p.append(v.reshape(bsz, seq, nh, HEAD_DIM))
            lf_p.append(lf_t.reshape(nh, bsz, seq).transpose(1, 2, 0))

            qs = _proj(us, w_in, 0, width, srows, 2048, BF16, q_norm_g[j])
            ks = _proj(us, w_in, width, width, srows, 2048, F32, k_norm_g[j])
            vs = _proj(us, w_in, 2 * width, width, srows, 2048, F32)
            gs = _proj(us, w_in, 3 * width, width, srows, 2048, F32)
            lfs = _logf_step(us, w_f, b_forget[j])[:dbs]
            heads = lambda a: a[:dbs].reshape(dbs, nh, HEAD_DIM)
            c_past = _page_cumsum(cache_logf, j, page_table)
            c_tot = c_past[:, n_pages - 1, PAGE_SIZE - 1, :]
            c_rows_s = c_past.reshape(dbs, n_pages, 1, PAGE_SIZE * nh)
            lane_head = lax.broadcasted_iota(jnp.int32, (nh, PAGE_SIZE * nh), 1) % nh
            row_head = lax.broadcasted_iota(jnp.int32, (nh, PAGE_SIZE * nh), 0)
            head_bias = jnp.where(lane_head == row_head, 0.0, -jnp.inf).astype(F32)
            os_ = _fox_attend_sample(heads(qs), heads(ks), heads(vs), heads(gs),
                                     lfs.reshape(dbs, nh, 1), c_tot.reshape(dbs, nh, 1),
                                     c_rows_s, head_bias, cache_k, cache_v, j, page_table)
            os_ = jnp.pad(os_.reshape(dbs, width), ((0, srows - dbs), (0, 0))).astype(BF16)
            xs = _mm_residual(os_, w_fox_out_b[j], xs, 1.0, srows, 2048, width)
            k_s.append(ks[:dbs].reshape(dbs, 1, nh, HEAD_DIM))
            v_s.append(vs[:dbs].reshape(dbs, 1, nh, HEAD_DIM))
            lf_s.append(lfs.reshape(dbs, 1, nh))
        xp = _ffn_half(xp, norm_g[i, 2], w_gu_b[i, 1], w_down_b[i, 1], **ptile)
        xs = _ffn_half(xs, norm_g[i, 2], w_gu_b[i, 1], w_down_b[i, 1], **stile)

    return (xp.reshape(bsz, seq, d), xs[:dbs].reshape(dbs, 1, d),
            jnp.stack(conv_p), jnp.stack(h_p), jnp.stack(k_p), jnp.stack(v_p), jnp.stack(lf_p),
            jnp.stack(conv_s), jnp.stack(h_s), jnp.stack(k_s), jnp.stack(v_s), jnp.stack(lf_s))
```

```python
import functools

import jax
import jax.numpy as jnp
from jax import lax
from jax.experimental import pallas as pl
from jax.experimental.pallas import tpu as pltpu

F32 = jnp.float32
BF16 = jnp.bfloat16

EPS = 1e-6
LRU_C = 8.0
LRU_BLOCK = 256
CONV_W = 4
HEAD_DIM = 128
PAGE_SIZE = 128
LOG2E = 1.4426950408889634

LANE = 128
SUBLANE = 8
BF16_ROWS = 16
VMEM_BYTES_V7X = 64 * 1024 * 1024
VMEM_LIMIT = VMEM_BYTES_V7X - 8 * 1024 * 1024

_NT_DIMS = (((1,), (1,)), ((), ()))


def _params(*semantics):
    return pltpu.CompilerParams(dimension_semantics=semantics, vmem_limit_bytes=VMEM_LIMIT)


def _weight_spec(block, lead, index_map):
    nlead = len(lead)
    return pl.BlockSpec((None,) * nlead + tuple(block),
                        lambda *g: tuple(lead) + tuple(index_map(*g)))


def _resident_spec(block, index_map):
    return pl.BlockSpec(block, index_map, pipeline_mode=pl.Buffered(1))


def _rms_body(x_ref, g_ref, o_ref):
    x = x_ref[...]
    ms = jnp.mean(x * x, axis=-1, keepdims=True)
    o_ref[...] = (x * lax.rsqrt(ms + EPS) * g_ref[...]).astype(o_ref.dtype)


def _rmsnorm(x, g):
    m, d = x.shape
    tr = min(m, 256)
    return pl.pallas_call(
        _rms_body,
        grid=(m // tr,),
        in_specs=[pl.BlockSpec((tr, d), lambda i: (i, 0)),
                  pl.BlockSpec((1, d), lambda i: (0, 0))],
        out_specs=pl.BlockSpec((tr, d), lambda i: (i, 0)),
        out_shape=jax.ShapeDtypeStruct((m, d), BF16),
        compiler_params=_params("parallel"),
        name="rmsnorm",
    )(x, g.reshape(1, d))


def _swiglu_body(a_ref, wg_ref, wu_ref, o_ref):
    a = a_ref[...]
    g = jnp.dot(a, wg_ref[...].astype(BF16), preferred_element_type=F32)
    u = jnp.dot(a, wu_ref[...].astype(BF16), preferred_element_type=F32)
    o_ref[...] = (jax.nn.silu(g) * u).astype(o_ref.dtype)


def _ffn_up(n, w_gu, lead, tm, tn):
    m, d = n.shape
    dff = w_gu.shape[-1] // 2
    nj = dff // tn
    return pl.pallas_call(
        _swiglu_body,
        grid=(m // tm, nj),
        in_specs=[_resident_spec((tm, d), lambda i, j: (i, 0)),
                  _weight_spec((d, tn), lead, lambda i, j: (0, j)),
                  _weight_spec((d, tn), lead, lambda i, j: (0, j + nj))],
        out_specs=pl.BlockSpec((tm, tn), lambda i, j: (i, j)),
        out_shape=jax.ShapeDtypeStruct((m, dff), BF16),
        compiler_params=_params("parallel", "parallel"),
        name="ffn_up",
    )(n, w_gu, w_gu)


def _mm_res_body(a_ref, w_ref, r_ref, o_ref, *scratch, nk, scale):
    part = jnp.dot(a_ref[...], w_ref[...].astype(BF16), preferred_element_type=F32)
    if nk == 1:
        o_ref[...] = r_ref[...] + scale * part
        return
    acc_ref, = scratch
    k = pl.program_id(2)

    @pl.when(k == 0)
    def _():
        acc_ref[...] = part

    @pl.when(jnp.logical_and(k > 0, k < nk - 1))
    def _():
        acc_ref[...] += part

    @pl.when(k == nk - 1)
    def _():
        o_ref[...] = r_ref[...] + scale * (acc_ref[...] + part)


def _mm_residual(a, w, lead, res, scale, tm, tn, tk):
    m, kdim = a.shape
    n = w.shape[-1]
    nk = kdim // tk
    scratch = [pltpu.VMEM((tm, tn), F32)] if nk > 1 else []
    lhs_spec = _resident_spec if nk == 1 else pl.BlockSpec
    return pl.pallas_call(
        functools.partial(_mm_res_body, nk=nk, scale=scale),
        grid=(m // tm, n // tn, nk),
        in_specs=[lhs_spec((tm, tk), lambda i, j, k: (i, k)),
                  _weight_spec((tk, tn), lead, lambda i, j, k: (k, j)),
                  pl.BlockSpec((tm, tn), lambda i, j, k: (i, j))],
        out_specs=pl.BlockSpec((tm, tn), lambda i, j, k: (i, j)),
        out_shape=jax.ShapeDtypeStruct((m, n), F32),
        scratch_shapes=scratch,
        compiler_params=_params("parallel", "parallel", "arbitrary"),
        name="mm_residual",
    )(a, w, res)


def _proj_body(a_ref, w_ref, *rest, tn, head_norm):
    acc = jnp.dot(a_ref[...], w_ref[...].astype(BF16), preferred_element_type=F32)
    if head_norm:
        g_ref, out_refs = rest[0], rest[1:]
        g = g_ref[...]
        for hh in range(tn // HEAD_DIM):
            sl = slice(hh * HEAD_DIM, (hh + 1) * HEAD_DIM)
            blk = acc[:, sl]
            ms = jnp.mean(blk * blk, axis=-1, keepdims=True)
            y = blk * lax.rsqrt(ms + EPS) * g
            for o_ref in out_refs:
                o_ref[:, sl] = y.astype(o_ref.dtype)
    else:
        for o_ref in rest:
            o_ref[...] = acc.astype(o_ref.dtype)


def _proj(a, w, lead, col0, ncols, tm, tn, out_dtypes, head_gain=None):
    m, kdim = a.shape
    joff = col0 // tn
    in_specs = [_resident_spec((tm, kdim), lambda i, j: (i, 0)),
                _weight_spec((kdim, tn), lead, lambda i, j: (0, j + joff))]
    args = [a, w]
    if head_gain is not None:
        in_specs.append(pl.BlockSpec((1, HEAD_DIM), lambda i, j: (0, 0)))
        args.append(head_gain.reshape(1, HEAD_DIM))
    outs = pl.pallas_call(
        functools.partial(_proj_body, tn=tn, head_norm=head_gain is not None),
        grid=(m // tm, ncols // tn),
        in_specs=in_specs,
        out_specs=[pl.BlockSpec((tm, tn), lambda i, j: (i, j)) for _ in out_dtypes],
        out_shape=[jax.ShapeDtypeStruct((m, ncols), dt) for dt in out_dtypes],
        compiler_params=_params("parallel", "parallel"),
        name="proj",
    )(*args)
    return outs if len(outs) > 1 else outs[0]


def _lru_gates(xc, wa_ref, wi_ref, ba, bi, lam, nblk):
    xcb = xc.astype(BF16)
    a_parts, b_parts = [], []
    for blk in range(nblk):
        sl = slice(blk * LRU_BLOCK, (blk + 1) * LRU_BLOCK)
        gate_r = jnp.dot(xcb[:, sl], wa_ref[blk], preferred_element_type=F32) + ba[:, sl]
        gate_i = jnp.dot(xcb[:, sl], wi_ref[blk], preferred_element_type=F32) + bi[:, sl]
        r = jax.nn.sigmoid(gate_r)
        ig = jax.nn.sigmoid(gate_i)
        log_a = -LRU_C * r * jax.nn.softplus(-lam[:, sl])
        a_parts.append(jnp.exp(log_a))
        th = jnp.tanh(log_a)
        b_parts.append(jnp.sqrt(-2.0 * th / (1.0 - th)) * (ig * xc[:, sl]))
    return a_parts, b_parts


def _lru_scan_body(xr_ref, y_ref, wa_ref, wi_ref, ba_ref, bi_ref, lam_ref, cw_ref, cb_ref,
                   o_ref, hl_ref, cbuf_ref, xpad, a_scr, b_scr, h_scr, *, tt, cw_cols):
    t = pl.program_id(2)
    nblk = cw_cols // LRU_BLOCK

    @pl.when(t == 0)
    def _():
        xpad[0:SUBLANE, :] = jnp.zeros((SUBLANE, cw_cols), F32)
        h_scr[...] = jnp.zeros((SUBLANE, cw_cols), F32)

    xr = xr_ref[...]
    xpad[SUBLANE:SUBLANE + tt, :] = xr
    cw = cw_ref[...]
    xc = cb_ref[...] + cw[0:1] * xpad[SUBLANE - 3:SUBLANE - 3 + tt, :]
    xc = xc + cw[1:2] * xpad[SUBLANE - 2:SUBLANE - 2 + tt, :]
    xc = xc + cw[2:3] * xpad[SUBLANE - 1:SUBLANE - 1 + tt, :]
    xc = xc + cw[3:4] * xr
    a_parts, b_parts = _lru_gates(xc, wa_ref, wi_ref, ba_ref[...], bi_ref[...], lam_ref[...], nblk)
    for blk in range(nblk):
        sl = slice(blk * LRU_BLOCK, (blk + 1) * LRU_BLOCK)
        a_scr[:, sl] = a_parts[blk]
        b_scr[:, sl] = b_parts[blk]

    row = lax.broadcasted_iota(jnp.int32, (SUBLANE, cw_cols), 0)

    def group(gi, h):
        off = pl.multiple_of(gi * SUBLANE, SUBLANE)
        a = a_scr[pl.ds(off, SUBLANE), :]
        b = b_scr[pl.ds(off, SUBLANE), :]
        for s in (1, 2, 4):
            a_sh = jnp.where(row >= s, pltpu.roll(a, s, 0), 1.0)
            b_sh = jnp.where(row >= s, pltpu.roll(b, s, 0), 0.0)
            b = a * b_sh + b
            a = a * a_sh
        hcur = a * h + b
        a_scr[pl.ds(off, SUBLANE), :] = hcur
        return jnp.broadcast_to(hcur[SUBLANE - 1:SUBLANE, :], (SUBLANE, cw_cols))

    h = lax.fori_loop(0, tt // SUBLANE, group, h_scr[...], unroll=4)
    h_scr[...] = h
    xpad[0:SUBLANE, :] = xpad[tt:tt + SUBLANE, :]
    o_ref[...] = (jax.nn.gelu(y_ref[...]) * a_scr[...]).astype(o_ref.dtype)

    @pl.when(t == pl.num_programs(2) - 1)
    def _():
        hl_ref[0] = h[0:1, :]
        cbuf_ref[0] = xr_ref[tt - (CONV_W - 1):tt, :]


def _lru_prompt(proj, bsz, seq, wa, wi, ba, bi, lam, cw, cb, tt=512, cw_cols=1024):
    d = proj.shape[1] // 2
    nc = d // cw_cols
    nt = seq // tt
    gpb = cw_cols // LRU_BLOCK
    vec = lambda: pl.BlockSpec((1, cw_cols), lambda b, c, t: (0, c))
    return pl.pallas_call(
        functools.partial(_lru_scan_body, tt=tt, cw_cols=cw_cols),
        grid=(bsz, nc, nt),
        in_specs=[pl.BlockSpec((tt, cw_cols), lambda b, c, t: (b * nt + t, nc + c)),
                  pl.BlockSpec((tt, cw_cols), lambda b, c, t: (b * nt + t, c)),
                  pl.BlockSpec((gpb, LRU_BLOCK, LRU_BLOCK), lambda b, c, t: (c, 0, 0)),
                  pl.BlockSpec((gpb, LRU_BLOCK, LRU_BLOCK), lambda b, c, t: (c, 0, 0)),
                  vec(), vec(), vec(),
                  pl.BlockSpec((CONV_W, cw_cols), lambda b, c, t: (0, c)),
                  vec()],
        out_specs=[pl.BlockSpec((tt, cw_cols), lambda b, c, t: (b * nt + t, c)),
                   pl.BlockSpec((1, 1, cw_cols), lambda b, c, t: (b, 0, c)),
                   pl.BlockSpec((1, CONV_W - 1, cw_cols), lambda b, c, t: (b, 0, c))],
        out_shape=[jax.ShapeDtypeStruct((bsz * seq, d), BF16),
                   jax.ShapeDtypeStruct((bsz, 1, d), F32),
                   jax.ShapeDtypeStruct((bsz, CONV_W - 1, d), F32)],
        scratch_shapes=[pltpu.VMEM((tt + SUBLANE, cw_cols), F32),
                        pltpu.VMEM((tt, cw_cols), F32),
                        pltpu.VMEM((tt, cw_cols), F32),
                        pltpu.VMEM((SUBLANE, cw_cols), F32)],
        compiler_params=_params("parallel", "parallel", "arbitrary"),
        name="lru_scan",
    )(proj, proj, wa, wi, ba.reshape(1, d), bi.reshape(1, d), lam.reshape(1, d), cw, cb.reshape(1, d))


def _lru_step_body(xr_ref, y_ref, b0_ref, b1_ref, b2_ref, h0_ref, wa_ref, wi_ref, ba_ref, bi_ref,
                   lam_ref, cw_ref, cb_ref, o_ref, h_ref, *, cw_cols):
    nblk = cw_cols // LRU_BLOCK
    cw = cw_ref[...]
    xr = xr_ref[...]
    xc = cb_ref[...] + cw[0:1] * b0_ref[...]
    xc = xc + cw[1:2] * b1_ref[...]
    xc = xc + cw[2:3] * b2_ref[...]
    xc = xc + cw[3:4] * xr
    a_parts, b_parts = _lru_gates(xc, wa_ref, wi_ref, ba_ref[...], bi_ref[...], lam_ref[...], nblk)
    y = y_ref[...]
    for blk in range(nblk):
        sl = slice(blk * LRU_BLOCK, (blk + 1) * LRU_BLOCK)
        h = a_parts[blk] * h0_ref[:, sl] + b_parts[blk]
        h_ref[:, sl] = h
        o_ref[:, sl] = (jax.nn.gelu(y[:, sl]) * h).astype(o_ref.dtype)


def _lru_step(proj, bufs, h0, wa, wi, ba, bi, lam, cw, cb, cw_cols=1024):
    rows = proj.shape[0]
    d = proj.shape[1] // 2
    nc = d // cw_cols
    gpb = cw_cols // LRU_BLOCK
    act = lambda off: pl.BlockSpec((rows, cw_cols), lambda c: (0, c + off))
    vec = lambda: pl.BlockSpec((1, cw_cols), lambda c: (0, c))
    return pl.pallas_call(
        functools.partial(_lru_step_body, cw_cols=cw_cols),
        grid=(nc,),
        in_specs=[act(nc), act(0), act(0), act(0), act(0), act(0),
                  pl.BlockSpec((gpb, LRU_BLOCK, LRU_BLOCK), lambda c: (c, 0, 0)),
                  pl.BlockSpec((gpb, LRU_BLOCK, LRU_BLOCK), lambda c: (c, 0, 0)),
                  vec(), vec(), vec(),
                  pl.BlockSpec((CONV_W, cw_cols), lambda c: (0, c)),
                  vec()],
        out_specs=[act(0), act(0)],
        out_shape=[jax.ShapeDtypeStruct((rows, d), BF16),
                   jax.ShapeDtypeStruct((rows, d), F32)],
        compiler_params=_params("parallel"),
        name="lru_step",
    )(proj, proj, bufs[0], bufs[1], bufs[2], h0, wa, wi,
      ba.reshape(1, d), bi.reshape(1, d), lam.reshape(1, d), cw, cb.reshape(1, d))


def _split3(x):
    hi = x.astype(BF16)
    r1 = x - hi.astype(F32)
    mid = r1.astype(BF16)
    lo = (r1 - mid.astype(F32)).astype(BF16)
    return hi, mid, lo


def _dot_x_ones(x, ones_rhs):
    hi, mid, lo = _split3(x)
    d = lambda p: jnp.dot(p, ones_rhs, preferred_element_type=F32)
    return d(hi) + d(mid) + d(lo)


def _dot_ones_x(ones_lhs, x):
    hi, mid, lo = _split3(x)
    d = lambda p: jnp.dot(ones_lhs, p, preferred_element_type=F32)
    return d(hi) + d(mid) + d(lo)


def _logf_body(wft_ref, n_ref, bf_ref, u_ref, lf_ref, c_ref, carry, *, tt):
    t = pl.program_id(1)

    @pl.when(t == 0)
    def _():
        carry[...] = jnp.zeros(carry.shape, F32)

    f = lax.dot_general(wft_ref[...], n_ref[...], _NT_DIMS, preferred_element_type=F32)
    lf = jax.nn.log_sigmoid(f + bf_ref[...])
    lf_ref[...] = lf
    u = u_ref[...]
    c = carry[...]
    nh = lf.shape[0]
    for kk in range(tt // LANE):
        sl = slice(kk * LANE, (kk + 1) * LANE)
        cs = _dot_x_ones(lf[:, sl], u) + c
        c_ref[:, sl] = cs
        c = jnp.broadcast_to(cs[:, LANE - 1:LANE], (nh, LANE))
    carry[...] = c


def _logf_prompt(n, wft, b_f, bsz, seq, tt=512):
    m, d = n.shape
    nh = wft.shape[0]
    nt = seq // tt
    upper = (lax.broadcasted_iota(jnp.int32, (LANE, LANE), 0)
             <= lax.broadcasted_iota(jnp.int32, (LANE, LANE), 1)).astype(BF16)
    return pl.pallas_call(
        functools.partial(_logf_body, tt=tt),
        grid=(bsz, nt),
        in_specs=[pl.BlockSpec((nh, d), lambda b, t: (0, 0)),
                  pl.BlockSpec((tt, d), lambda b, t: (b * nt + t, 0)),
                  pl.BlockSpec((nh, 1), lambda b, t: (0, 0)),
                  pl.BlockSpec((LANE, LANE), lambda b, t: (0, 0))],
        out_specs=[pl.BlockSpec((nh, tt), lambda b, t: (0, b * nt + t)),
                   pl.BlockSpec((nh, tt), lambda b, t: (0, b * nt + t))],
        out_shape=[jax.ShapeDtypeStruct((nh, m), F32),
                   jax.ShapeDtypeStruct((nh, m), F32)],
        scratch_shapes=[pltpu.VMEM((nh, LANE), F32)],
        compiler_params=_params("parallel", "arbitrary"),
        name="logf_prompt",
    )(wft, n, b_f.reshape(nh, 1), upper)


def _flash_body(q_ref, k_ref, v_ref, g_ref, c_ref, o_ref, *, tq, nq, scale):
    c2 = c_ref[0] * LOG2E
    row = lax.broadcasted_iota(jnp.int32, (tq, tq), 0)
    col = lax.broadcasted_iota(jnp.int32, (tq, tq), 1)
    visible = col <= row

    def scores(q, lo, hi):
        s = lax.dot_general(q, k_ref[lo:hi, :], _NT_DIMS, preferred_element_type=F32)
        return s * (scale * LOG2E) - c2[:, lo:hi]

    for qi in range(nq):
        lo, hi = qi * tq, (qi + 1) * tq
        q = q_ref[lo:hi, :]
        s_diag = jnp.where(visible, scores(q, lo, hi), -jnp.inf)
        m = jnp.max(s_diag, axis=-1, keepdims=True)
        if qi > 0:
            s_past = scores(q, 0, lo)
            m = jnp.maximum(m, jnp.max(s_past, axis=-1, keepdims=True))
        p_diag = jnp.exp2(s_diag - m)
        l = jnp.sum(p_diag, axis=-1, keepdims=True)
        acc = jnp.dot(p_diag.astype(BF16), v_ref[lo:hi, :], preferred_element_type=F32)
        if qi > 0:
            p_past = jnp.exp2(s_past - m)
            l = l + jnp.sum(p_past, axis=-1, keepdims=True)
            acc = acc + jnp.dot(p_past.astype(BF16), v_ref[0:lo, :], preferred_element_type=F32)
        o_ref[lo:hi, :] = (acc / l * jax.nn.sigmoid(g_ref[lo:hi, :])).astype(o_ref.dtype)


def _fox_attend_prompt(q, k, v, g, c_rows, bsz, seq, tq=512):
    m, width = q.shape
    nh = width // HEAD_DIM
    tile = lambda: pl.BlockSpec((seq, HEAD_DIM), lambda b, h: (b, h))
    return pl.pallas_call(
        functools.partial(_flash_body, tq=tq, nq=seq // tq, scale=HEAD_DIM ** -0.5),
        grid=(bsz, nh),
        in_specs=[tile(), tile(), tile(), tile(),
                  pl.BlockSpec((1, 1, seq), lambda b, h: (b * nh + h, 0, 0))],
        out_specs=tile(),
        out_shape=jax.ShapeDtypeStruct((m, width), BF16),
        compiler_params=_params("parallel", "parallel"),
        name="fox_prompt_attn",
    )(q, k, v, g, c_rows)


def _page_cumsum_body(pt_ref, *refs, pages):
    lf_refs = refs[:pages]
    ltri_ref, o_ref, carry = refs[pages], refs[pages + 1], refs[pages + 2]
    j = pl.program_id(1)

    @pl.when(j == 0)
    def _():
        carry[...] = jnp.zeros(carry.shape, F32)

    ltri = ltri_ref[...]
    c = carry[...]
    for p in range(pages):
        cs = _dot_ones_x(ltri, lf_refs[p][0, 0]) + c
        o_ref[0, p] = cs
        c = cs[PAGE_SIZE - 1:PAGE_SIZE, :]
    carry[...] = c


def _page_cumsum(cache_logf, layer, page_table, pages=8):
    dbs, n_pages = page_table.shape
    nh = cache_logf.shape[-1]
    lower = (lax.broadcasted_iota(jnp.int32, (PAGE_SIZE, PAGE_SIZE), 1)
             <= lax.broadcasted_iota(jnp.int32, (PAGE_SIZE, PAGE_SIZE), 0)).astype(BF16)

    def page_spec(p):
        return pl.BlockSpec((1, 1, PAGE_SIZE, nh),
                            lambda b, j, pt: (layer, pt[b, j * pages + p], 0, 0))

    grid_spec = pltpu.PrefetchScalarGridSpec(
        num_scalar_prefetch=1,
        grid=(dbs, n_pages // pages),
        in_specs=[page_spec(p) for p in range(pages)]
        + [pl.BlockSpec((PAGE_SIZE, PAGE_SIZE), lambda b, j, pt: (0, 0))],
        out_specs=pl.BlockSpec((1, pages, PAGE_SIZE, nh), lambda b, j, pt: (b, j, 0, 0)),
        scratch_shapes=[pltpu.VMEM((1, nh), F32)])
    return pl.pallas_call(
        functools.partial(_page_cumsum_body, pages=pages),
        grid_spec=grid_spec,
        out_shape=jax.ShapeDtypeStruct((dbs, n_pages, PAGE_SIZE, nh), F32),
        compiler_params=_params("parallel", "arbitrary"),
        name="page_cumsum",
    )(page_table, *([cache_logf] * pages), lower)


def _decode_body(pt_ref, *refs, pages, scale):
    q_ref = refs[0]
    k_refs = refs[1:1 + pages]
    v_refs = refs[1 + pages:1 + 2 * pages]
    (c_ref, hb_ref, kn_ref, vn_ref, lfn_ref, ctot_ref, g_ref,
     o_ref, m_scr, l_scr, acc_scr) = refs[1 + 2 * pages:]
    j = pl.program_id(1)

    @pl.when(j == 0)
    def _():
        m_scr[...] = jnp.full(m_scr.shape, -jnp.inf, F32)
        l_scr[...] = jnp.zeros(l_scr.shape, F32)
        acc_scr[...] = jnp.zeros(acc_scr.shape, F32)

    q = q_ref[0]
    hb = hb_ref[...]
    nh = q.shape[0]
    for p in range(pages):
        k = k_refs[p][0, 0].reshape(PAGE_SIZE * nh, HEAD_DIM).astype(BF16)
        v = v_refs[p][0, 0].reshape(PAGE_SIZE * nh, HEAD_DIM).astype(BF16)
        s = lax.dot_general(q, k, _NT_DIMS, preferred_element_type=F32) * scale
        s = s - c_ref[0, p] + hb
        m_prev = m_scr[...]
        m_new = jnp.maximum(m_prev, jnp.max(s, axis=-1, keepdims=True))
        alpha = jnp.exp(m_prev - m_new)
        pr = jnp.exp(s - m_new)
        l_scr[...] = alpha * l_scr[...] + jnp.sum(pr, axis=-1, keepdims=True)
        acc_scr[...] = alpha * acc_scr[...] + jnp.dot(pr.astype(BF16), v, preferred_element_type=F32)
        m_scr[...] = m_new

    @pl.when(j == pl.num_programs(1) - 1)
    def _():
        kn = kn_ref[0].astype(BF16).astype(F32)
        s_self = jnp.sum(q.astype(F32) * kn, axis=-1, keepdims=True) * scale
        s_self = s_self - (ctot_ref[0] + lfn_ref[0])
        m_prev = m_scr[...]
        m_new = jnp.maximum(m_prev, s_self)
        alpha = jnp.exp(m_prev - m_new)
        p_self = jnp.exp(s_self - m_new)
        l = alpha * l_scr[...] + p_self
        acc = alpha * acc_scr[...] + p_self * vn_ref[0]
        o_ref[0] = (acc / l) * jax.nn.sigmoid(g_ref[0])


def _fox_attend_sample(q, k_new, v_new, g, lf_new, c_tot, c_rows, head_bias,
                       cache_k, cache_v, layer, page_table, pages=4):
    dbs, n_pages = page_table.shape
    nh = q.shape[1]
    row_w = PAGE_SIZE * nh

    def page_spec(p):
        return pl.BlockSpec((1, 1, PAGE_SIZE, nh, HEAD_DIM),
                            lambda b, j, pt: (layer, pt[b, j * pages + p], 0, 0, 0))

    tok = lambda: pl.BlockSpec((1, nh, HEAD_DIM), lambda b, j, pt: (b, 0, 0))
    col = lambda: pl.BlockSpec((1, nh, 1), lambda b, j, pt: (b, 0, 0))
    grid_spec = pltpu.PrefetchScalarGridSpec(
        num_scalar_prefetch=1,
        grid=(dbs, n_pages // pages),
        in_specs=[tok()]
        + [page_spec(p) for p in range(pages)]
        + [page_spec(p) for p in range(pages)]
        + [pl.BlockSpec((1, pages, 1, row_w), lambda b, j, pt: (b, j, 0, 0)),
           pl.BlockSpec((nh, row_w), lambda b, j, pt: (0, 0)),
           tok(), tok(), col(), col(), tok()],
        out_specs=tok(),
        scratch_shapes=[pltpu.VMEM((nh, 1), F32), pltpu.VMEM((nh, 1), F32),
                        pltpu.VMEM((nh, HEAD_DIM), F32)])
    return pl.pallas_call(
        functools.partial(_decode_body, pages=pages, scale=HEAD_DIM ** -0.5),
        grid_spec=grid_spec,
        out_shape=jax.ShapeDtypeStruct((dbs, nh, HEAD_DIM), F32),
        compiler_params=_params("parallel", "arbitrary"),
        name="fox_sample_attn",
    )(page_table, q, *([cache_k] * pages), *([cache_v] * pages),
      c_rows, head_bias, k_new, v_new, lf_new, c_tot, g)


def _logf_step_body(n_ref, wf_ref, bf_ref, o_ref):
    f = jnp.dot(n_ref[...], wf_ref[...], preferred_element_type=F32)
    o_ref[...] = jax.nn.log_sigmoid(f + bf_ref[...])


def _logf_step(n, wf, b_f):
    rows = n.shape[0]
    nh = wf.shape[1]
    return pl.pallas_call(
        _logf_step_body,
        out_shape=jax.ShapeDtypeStruct((rows, nh), F32),
        name="logf_step",
    )(n, wf, b_f.reshape(1, nh))


def _ffn_half(x, g, w_gu, w_down, lead, tm, tn_up, tm_down, tn_down, tk_down):
    n = _rmsnorm(x, g)
    hidden = _ffn_up(n, w_gu, lead, tm, tn_up)
    return _mm_residual(hidden, w_down, lead, x, 0.5, tm_down, tn_down, tk_down)


def kernel(x_prompt, x_sample, state_conv, state_h, cache_k, cache_v, cache_logf, page_table,
           norm_g, w_ffn_gu, w_ffn_down, w_rec_in, conv_w, conv_b, w_gate_a, b_gate_a,
           w_gate_i, b_gate_i, lru_lambda, w_rec_out, w_fox_in, b_forget, q_norm_g, k_norm_g,
           w_fox_out):
    bsz, seq, d = x_prompt.shape
    dbs, dseq, _ = x_sample.shape
    assert dseq == 1, "sample group decodes one token per sequence"
    depth = norm_g.shape[0]
    nh = b_forget.shape[1]
    width = nh * HEAD_DIM
    dff = w_ffn_down.shape[2]
    n_pages = page_table.shape[1]
    srows = BF16_ROWS
    assert dbs <= srows

    w_ga_b = w_gate_a.astype(BF16)
    w_gi_b = w_gate_i.astype(BF16)

    xp = x_prompt.reshape(bsz * seq, d)
    xs = jnp.pad(x_sample.reshape(dbs, d), ((0, srows - dbs), (0, 0)))

    ptile = dict(tm=2048, tn_up=256, tm_down=1024, tn_down=256, tk_down=dff)
    stile = dict(tm=srows, tn_up=512, tm_down=srows, tn_down=512, tk_down=dff // 2)
    pm, pn = 2048, 256
    sn = 512

    conv_p, h_p, k_p, v_p, lf_p = [], [], [], [], []
    conv_s, h_s, k_s, v_s, lf_s = [], [], [], [], []
    for i in range(depth):
        j = i // 2
        xp = _ffn_half(xp, norm_g[i, 0], w_ffn_gu, w_ffn_down, (i, 0), **ptile)
        xs = _ffn_half(xs, norm_g[i, 0], w_ffn_gu, w_ffn_down, (i, 0), **stile)
        up = _rmsnorm(xp, norm_g[i, 1])
        us = _rmsnorm(xs, norm_g[i, 1])
        if i % 2 == 0:
            lru_w = (w_ga_b[j], w_gi_b[j], b_gate_a[j], b_gate_i[j], lru_lambda[j], conv_w[j], conv_b[j])
            proj_p = _proj(up, w_rec_in, (j,), 0, 2 * d, pm, pn, [F32])
            gated_p, hl, cbuf = _lru_prompt(proj_p, bsz, seq, *lru_w)
            xp = _mm_residual(gated_p, w_rec_out, (j,), xp, 1.0, pm, pn, d)
            conv_p.append(cbuf)
            h_p.append(hl.reshape(bsz, d))

            proj_s = _proj(us, w_rec_in, (j,), 0, 2 * d, srows, sn, [F32])
            pad_rows = lambda a: jnp.pad(a, ((0, srows - dbs), (0, 0)))
            bufs = [pad_rows(state_conv[j, :, r, :]) for r in range(CONV_W - 1)]
            gated_s, h_new = _lru_step(proj_s, bufs, pad_rows(state_h[j]), *lru_w)
            xs = _mm_residual(gated_s, w_rec_out, (j,), xs, 1.0, srows, sn, d)
            conv_s.append(jnp.stack([state_conv[j, :, 1, :], state_conv[j, :, 2, :],
                                     proj_s[:dbs, d:]], axis=1))
            h_s.append(h_new[:dbs])
        else:
            w_f = w_fox_in[j, :, 4 * width:].astype(BF16)
            q = _proj(up, w_fox_in, (j,), 0, width, pm, pn, [BF16], q_norm_g[j])
            k, k_b = _proj(up, w_fox_in, (j,), width, width, pm, pn, [F32, BF16], k_norm_g[j])
            v, v_b = _proj(up, w_fox_in, (j,), 2 * width, width, pm, pn, [F32, BF16])
            g = _proj(up, w_fox_in, (j,), 3 * width, width, pm, pn, [F32])
            lf_t, c_t = _logf_prompt(up, w_f.T, b_forget[j], bsz, seq)
            c_rows = c_t.reshape(nh, bsz, seq).transpose(1, 0, 2).reshape(bsz * nh, 1, seq)
            o = _fox_attend_prompt(q, k_b, v_b, g, c_rows, bsz, seq)
            xp = _mm_residual(o, w_fox_out, (j,), xp, 1.0, pm, pn, width)
            k_p.append(k.reshape(bsz, seq, nh, HEAD_DIM))
            v_p.append(v.reshape(bsz, seq, nh, HEAD_DIM))
            lf_p.append(lf_t.reshape(nh, bsz, seq).transpose(1, 2, 0))

            qs = _proj(us, w_fox_in, (j,), 0, width, srows, sn, [BF16], q_norm_g[j])
            ks = _proj(us, w_fox_in, (j,), width, width, srows, sn, [F32], k_norm_g[j])
            vs = _proj(us, w_fox_in, (j,), 2 * width, width, srows, sn, [F32])
            gs = _proj(us, w_fox_in, (j,), 3 * width, width, srows, sn, [F32])
            lfs = _logf_step(us, w_f, b_forget[j])[:dbs]
            heads = lambda a: a[:dbs].reshape(dbs, nh, HEAD_DIM)
            c_past = _page_cumsum(cache_logf, j, page_table)
            c_tot = c_past[:, n_pages - 1, PAGE_SIZE - 1, :]
            c_rows_s = c_past.reshape(dbs, n_pages, 1, PAGE_SIZE * nh)
            lane_head = lax.broadcasted_iota(jnp.int32, (nh, PAGE_SIZE * nh), 1) % nh
            row_head = lax.broadcasted_iota(jnp.int32, (nh, PAGE_SIZE * nh), 0)
            head_bias = jnp.where(lane_head == row_head, 0.0, -jnp.inf).astype(F32)
            os_ = _fox_attend_sample(heads(qs), heads(ks), heads(vs), heads(gs),
                                     lfs.reshape(dbs, nh, 1), c_tot.reshape(dbs, nh, 1),
                                     c_rows_s, head_bias, cache_k, cache_v, j, page_table)
            os_ = jnp.pad(os_.reshape(dbs, width), ((0, srows - dbs), (0, 0))).astype(BF16)
            xs = _mm_residual(os_, w_fox_out, (j,), xs, 1.0, srows, sn, width)
            k_s.append(ks[:dbs].reshape(dbs, 1, nh, HEAD_DIM))
            v_s.append(vs[:dbs].reshape(dbs, 1, nh, HEAD_DIM))
            lf_s.append(lfs.reshape(dbs, 1, nh))
        xp = _ffn_half(xp, norm_g[i, 2], w_ffn_gu, w_ffn_down, (i, 1), **ptile)
        xs = _ffn_half(xs, norm_g[i, 2], w_ffn_gu, w_ffn_down, (i, 1), **stile)

    return (xp.reshape(bsz, seq, d), xs[:dbs].reshape(dbs, 1, d),
            jnp.stack(conv_p), jnp.stack(h_p), jnp.stack(k_p), jnp.stack(v_p), jnp.stack(lf_p),
            jnp.stack(conv_s), jnp.stack(h_s), jnp.stack(k_s), jnp.stack(v_s), jnp.stack(lf_s))
```

```python
import functools

import jax
import jax.numpy as jnp
from jax import lax
from jax.experimental import pallas as pl
from jax.experimental.pallas import tpu as pltpu

F32 = jnp.float32
BF16 = jnp.bfloat16

EPS = 1e-6
LRU_C = 8.0
LRU_BLOCK = 256
CONV_W = 4
HEAD_DIM = 128
PAGE_SIZE = 128
LOG2E = 1.4426950408889634

LANE = 128
SUBLANE = 8
BF16_ROWS = 16
VMEM_BYTES_V7X = 64 * 1024 * 1024
VMEM_LIMIT = VMEM_BYTES_V7X - 8 * 1024 * 1024

_NT_DIMS = (((1,), (1,)), ((), ()))


def _params(*semantics):
    return pltpu.CompilerParams(dimension_semantics=semantics, vmem_limit_bytes=VMEM_LIMIT)


def _weight_spec(block, lead, index_map):
    nlead = len(lead)
    return pl.BlockSpec((None,) * nlead + tuple(block),
                        lambda *g: tuple(lead) + tuple(index_map(*g)))


def _rms_body(x_ref, g_ref, o_ref):
    x = x_ref[...]
    ms = jnp.mean(x * x, axis=-1, keepdims=True)
    o_ref[...] = (x * lax.rsqrt(ms + EPS) * g_ref[...]).astype(o_ref.dtype)


def _rmsnorm(x, g):
    m, d = x.shape
    tr = min(m, 256)
    return pl.pallas_call(
        _rms_body,
        grid=(m // tr,),
        in_specs=[pl.BlockSpec((tr, d), lambda i: (i, 0)),
                  pl.BlockSpec((1, d), lambda i: (0, 0))],
        out_specs=pl.BlockSpec((tr, d), lambda i: (i, 0)),
        out_shape=jax.ShapeDtypeStruct((m, d), BF16),
        compiler_params=_params("parallel"),
        name="rmsnorm",
    )(x, g.reshape(1, d))


def _swiglu_body(a_ref, wg_ref, wu_ref, o_ref):
    a = a_ref[...]
    g = jnp.dot(a, wg_ref[...].astype(BF16), preferred_element_type=F32)
    u = jnp.dot(a, wu_ref[...].astype(BF16), preferred_element_type=F32)
    o_ref[...] = (jax.nn.silu(g) * u).astype(o_ref.dtype)


def _ffn_up(n, w_gu, lead, tm, tn):
    m, d = n.shape
    dff = w_gu.shape[-1] // 2
    nj = dff // tn
    return pl.pallas_call(
        _swiglu_body,
        grid=(m // tm, nj),
        in_specs=[pl.BlockSpec((tm, d), lambda i, j: (i, 0)),
                  _weight_spec((d, tn), lead, lambda i, j: (0, j)),
                  _weight_spec((d, tn), lead, lambda i, j: (0, j + nj))],
        out_specs=pl.BlockSpec((tm, tn), lambda i, j: (i, j)),
        out_shape=jax.ShapeDtypeStruct((m, dff), BF16),
        compiler_params=_params("parallel", "parallel"),
        name="ffn_up",
    )(n, w_gu, w_gu)


def _mm_res_body(a_ref, w_ref, r_ref, o_ref, *rest, nk, scale, emit_w):
    w = w_ref[...].astype(BF16)
    if emit_w:
        rest[0][...] = w
        rest = rest[1:]
    part = jnp.dot(a_ref[...], w, preferred_element_type=F32)
    if nk == 1:
        o_ref[...] = r_ref[...] + scale * part
        return
    acc_ref, = rest
    k = pl.program_id(2)

    @pl.when(k == 0)
    def _():
        acc_ref[...] = part

    @pl.when(jnp.logical_and(k > 0, k < nk - 1))
    def _():
        acc_ref[...] += part

    @pl.when(k == nk - 1)
    def _():
        o_ref[...] = r_ref[...] + scale * (acc_ref[...] + part)


def _mm_residual(a, w, lead, res, scale, tm, tn, tk, emit_w=False, single_buffer_lhs=False):
    m, kdim = a.shape
    n = w.shape[-1]
    nk = kdim // tk
    assert not emit_w or m == tm
    scratch = [pltpu.VMEM((tm, tn), F32)] if nk > 1 else []
    lhs_mode = pl.Buffered(1) if single_buffer_lhs else None
    out_specs = [pl.BlockSpec((tm, tn), lambda i, j, k: (i, j))]
    out_shape = [jax.ShapeDtypeStruct((m, n), F32)]
    if emit_w:
        out_specs.append(pl.BlockSpec((tk, tn), lambda i, j, k: (k, j)))
        out_shape.append(jax.ShapeDtypeStruct((kdim, n), BF16))
    outs = pl.pallas_call(
        functools.partial(_mm_res_body, nk=nk, scale=scale, emit_w=emit_w),
        grid=(m // tm, n // tn, nk),
        in_specs=[pl.BlockSpec((tm, tk), lambda i, j, k: (i, k), pipeline_mode=lhs_mode),
                  _weight_spec((tk, tn), lead, lambda i, j, k: (k, j)),
                  pl.BlockSpec((tm, tn), lambda i, j, k: (i, j))],
        out_specs=out_specs,
        out_shape=out_shape,
        scratch_shapes=scratch,
        compiler_params=_params("parallel", "parallel", "arbitrary"),
        name="mm_residual",
    )(a, w, res)
    return outs if emit_w else outs[0]


def _proj_body(a_ref, w_ref, *rest, tn, head_norm, w_rows_out):
    w = w_ref[...].astype(BF16)
    if w_rows_out:
        acc = lax.dot_general(a_ref[...], w, _NT_DIMS, preferred_element_type=F32)
    else:
        acc = jnp.dot(a_ref[...], w, preferred_element_type=F32)
    if head_norm:
        g_ref, out_refs = rest[0], rest[1:]
        g = g_ref[...]
        for hh in range(tn // HEAD_DIM):
            sl = slice(hh * HEAD_DIM, (hh + 1) * HEAD_DIM)
            blk = acc[:, sl]
            ms = jnp.mean(blk * blk, axis=-1, keepdims=True)
            y = blk * lax.rsqrt(ms + EPS) * g
            for o_ref in out_refs:
                o_ref[:, sl] = y.astype(o_ref.dtype)
    else:
        for o_ref in rest:
            o_ref[...] = acc.astype(o_ref.dtype)


def _proj(a, w, lead, col0, ncols, tm, tn, out_dtypes, head_gain=None, w_rows_out=False,
          single_buffer_lhs=False):
    m, kdim = a.shape
    joff = col0 // tn
    if w_rows_out:
        w_spec = _weight_spec((tn, kdim), lead, lambda i, j: (j + joff, 0))
    else:
        w_spec = _weight_spec((kdim, tn), lead, lambda i, j: (0, j + joff))
    lhs_mode = pl.Buffered(1) if single_buffer_lhs else None
    in_specs = [pl.BlockSpec((tm, kdim), lambda i, j: (i, 0), pipeline_mode=lhs_mode), w_spec]
    args = [a, w]
    if head_gain is not None:
        in_specs.append(pl.BlockSpec((1, HEAD_DIM), lambda i, j: (0, 0)))
        args.append(head_gain.reshape(1, HEAD_DIM))
    outs = pl.pallas_call(
        functools.partial(_proj_body, tn=tn, head_norm=head_gain is not None,
                          w_rows_out=w_rows_out),
        grid=(m // tm, ncols // tn),
        in_specs=in_specs,
        out_specs=[pl.BlockSpec((tm, tn), lambda i, j: (i, j)) for _ in out_dtypes],
        out_shape=[jax.ShapeDtypeStruct((m, ncols), dt) for dt in out_dtypes],
        compiler_params=_params("parallel", "parallel"),
        name="proj",
    )(*args)
    return outs if len(outs) > 1 else outs[0]


def _lru_gates(xc, wa_ref, wi_ref, ba, bi, lam, nblk):
    xcb = xc.astype(BF16)
    a_parts, b_parts = [], []
    for blk in range(nblk):
        sl = slice(blk * LRU_BLOCK, (blk + 1) * LRU_BLOCK)
        gate_r = jnp.dot(xcb[:, sl], wa_ref[blk], preferred_element_type=F32) + ba[:, sl]
        gate_i = jnp.dot(xcb[:, sl], wi_ref[blk], preferred_element_type=F32) + bi[:, sl]
        r = jax.nn.sigmoid(gate_r)
        ig = jax.nn.sigmoid(gate_i)
        log_a = -LRU_C * r * jax.nn.softplus(-lam[:, sl])
        a_parts.append(jnp.exp(log_a))
        th = jnp.tanh(log_a)
        b_parts.append(jnp.sqrt(-2.0 * th / (1.0 - th)) * (ig * xc[:, sl]))
    return a_parts, b_parts


def _lru_scan_body(xr_ref, y_ref, wa_ref, wi_ref, ba_ref, bi_ref, lam_ref, cw_ref, cb_ref,
                   o_ref, hl_ref, cbuf_ref, xpad, a_scr, b_scr, h_scr, *, tt, cw_cols):
    t = pl.program_id(2)
    nblk = cw_cols // LRU_BLOCK

    @pl.when(t == 0)
    def _():
        xpad[0:SUBLANE, :] = jnp.zeros((SUBLANE, cw_cols), F32)
        h_scr[...] = jnp.zeros((SUBLANE, cw_cols), F32)

    xr = xr_ref[...]
    xpad[SUBLANE:SUBLANE + tt, :] = xr
    cw = cw_ref[...]
    xc = cb_ref[...] + cw[0:1] * xpad[SUBLANE - 3:SUBLANE - 3 + tt, :]
    xc = xc + cw[1:2] * xpad[SUBLANE - 2:SUBLANE - 2 + tt, :]
    xc = xc + cw[2:3] * xpad[SUBLANE - 1:SUBLANE - 1 + tt, :]
    xc = xc + cw[3:4] * xr
    a_parts, b_parts = _lru_gates(xc, wa_ref, wi_ref, ba_ref[...], bi_ref[...], lam_ref[...], nblk)
    for blk in range(nblk):
        sl = slice(blk * LRU_BLOCK, (blk + 1) * LRU_BLOCK)
        a_scr[:, sl] = a_parts[blk]
        b_scr[:, sl] = b_parts[blk]

    row = lax.broadcasted_iota(jnp.int32, (SUBLANE, cw_cols), 0)

    def group(gi, h):
        off = pl.multiple_of(gi * SUBLANE, SUBLANE)
        a = a_scr[pl.ds(off, SUBLANE), :]
        b = b_scr[pl.ds(off, SUBLANE), :]
        for s in (1, 2, 4):
            a_sh = jnp.where(row >= s, pltpu.roll(a, s, 0), 1.0)
            b_sh = jnp.where(row >= s, pltpu.roll(b, s, 0), 0.0)
            b = a * b_sh + b
            a = a * a_sh
        hcur = a * h + b
        a_scr[pl.ds(off, SUBLANE), :] = hcur
        return jnp.broadcast_to(hcur[SUBLANE - 1:SUBLANE, :], (SUBLANE, cw_cols))

    h = lax.fori_loop(0, tt // SUBLANE, group, h_scr[...], unroll=4)
    h_scr[...] = h
    xpad[0:SUBLANE, :] = xpad[tt:tt + SUBLANE, :]
    o_ref[...] = (jax.nn.gelu(y_ref[...]) * a_scr[...]).astype(o_ref.dtype)

    @pl.when(t == pl.num_programs(2) - 1)
    def _():
        hl_ref[0] = h[0:1, :]
        cbuf_ref[0] = xr_ref[tt - (CONV_W - 1):tt, :]


def _lru_prompt(proj, bsz, seq, wa, wi, ba, bi, lam, cw, cb, tt=512, cw_cols=1024):
    d = proj.shape[1] // 2
    nc = d // cw_cols
    nt = seq // tt
    gpb = cw_cols // LRU_BLOCK
    vec = lambda: pl.BlockSpec((1, cw_cols), lambda b, c, t: (0, c))
    return pl.pallas_call(
        functools.partial(_lru_scan_body, tt=tt, cw_cols=cw_cols),
        grid=(bsz, nc, nt),
        in_specs=[pl.BlockSpec((tt, cw_cols), lambda b, c, t: (b * nt + t, nc + c)),
                  pl.BlockSpec((tt, cw_cols), lambda b, c, t: (b * nt + t, c)),
                  pl.BlockSpec((gpb, LRU_BLOCK, LRU_BLOCK), lambda b, c, t: (c, 0, 0)),
                  pl.BlockSpec((gpb, LRU_BLOCK, LRU_BLOCK), lambda b, c, t: (c, 0, 0)),
                  vec(), vec(), vec(),
                  pl.BlockSpec((CONV_W, cw_cols), lambda b, c, t: (0, c)),
                  vec()],
        out_specs=[pl.BlockSpec((tt, cw_cols), lambda b, c, t: (b * nt + t, c)),
                   pl.BlockSpec((1, 1, cw_cols), lambda b, c, t: (b, 0, c)),
                   pl.BlockSpec((1, CONV_W - 1, cw_cols), lambda b, c, t: (b, 0, c))],
        out_shape=[jax.ShapeDtypeStruct((bsz * seq, d), BF16),
                   jax.ShapeDtypeStruct((bsz, 1, d), F32),
                   jax.ShapeDtypeStruct((bsz, CONV_W - 1, d), F32)],
        scratch_shapes=[pltpu.VMEM((tt + SUBLANE, cw_cols), F32),
                        pltpu.VMEM((tt, cw_cols), F32),
                        pltpu.VMEM((tt, cw_cols), F32),
                        pltpu.VMEM((SUBLANE, cw_cols), F32)],
        compiler_params=_params("parallel", "parallel", "arbitrary"),
        name="lru_scan",
    )(proj, proj, wa, wi, ba.reshape(1, d), bi.reshape(1, d), lam.reshape(1, d), cw, cb.reshape(1, d))


def _lru_step_body(xr_ref, y_ref, b0_ref, b1_ref, b2_ref, h0_ref, wa_ref, wi_ref, ba_ref, bi_ref,
                   lam_ref, cw_ref, cb_ref, o_ref, h_ref, *, cw_cols):
    nblk = cw_cols // LRU_BLOCK
    cw = cw_ref[...]
    xr = xr_ref[...]
    xc = cb_ref[...] + cw[0:1] * b0_ref[...]
    xc = xc + cw[1:2] * b1_ref[...]
    xc = xc + cw[2:3] * b2_ref[...]
    xc = xc + cw[3:4] * xr
    a_parts, b_parts = _lru_gates(xc, wa_ref, wi_ref, ba_ref[...], bi_ref[...], lam_ref[...], nblk)
    y = y_ref[...]
    for blk in range(nblk):
        sl = slice(blk * LRU_BLOCK, (blk + 1) * LRU_BLOCK)
        h = a_parts[blk] * h0_ref[:, sl] + b_parts[blk]
        h_ref[:, sl] = h
        o_ref[:, sl] = (jax.nn.gelu(y[:, sl]) * h).astype(o_ref.dtype)


def _lru_step(proj, bufs, h0, wa, wi, ba, bi, lam, cw, cb, cw_cols=1024):
    rows = proj.shape[0]
    d = proj.shape[1] // 2
    nc = d // cw_cols
    gpb = cw_cols // LRU_BLOCK
    act = lambda off: pl.BlockSpec((rows, cw_cols), lambda c: (0, c + off))
    vec = lambda: pl.BlockSpec((1, cw_cols), lambda c: (0, c))
    return pl.pallas_call(
        functools.partial(_lru_step_body, cw_cols=cw_cols),
        grid=(nc,),
        in_specs=[act(nc), act(0), act(0), act(0), act(0), act(0),
                  pl.BlockSpec((gpb, LRU_BLOCK, LRU_BLOCK), lambda c: (c, 0, 0)),
                  pl.BlockSpec((gpb, LRU_BLOCK, LRU_BLOCK), lambda c: (c, 0, 0)),
                  vec(), vec(), vec(),
                  pl.BlockSpec((CONV_W, cw_cols), lambda c: (0, c)),
                  vec()],
        out_specs=[act(0), act(0)],
        out_shape=[jax.ShapeDtypeStruct((rows, d), BF16),
                   jax.ShapeDtypeStruct((rows, d), F32)],
        compiler_params=_params("parallel"),
        name="lru_step",
    )(proj, proj, bufs[0], bufs[1], bufs[2], h0, wa, wi,
      ba.reshape(1, d), bi.reshape(1, d), lam.reshape(1, d), cw, cb.reshape(1, d))


def _split3(x):
    hi = x.astype(BF16)
    r1 = x - hi.astype(F32)
    mid = r1.astype(BF16)
    lo = (r1 - mid.astype(F32)).astype(BF16)
    return hi, mid, lo


def _dot_x_ones(x, ones_rhs):
    hi, mid, lo = _split3(x)
    d = lambda p: jnp.dot(p, ones_rhs, preferred_element_type=F32)
    return d(hi) + d(mid) + d(lo)


def _logf_body(wft_ref, n_ref, bf_ref, u_ref, lf_ref, c_ref, carry, *, tt):
    t = pl.program_id(1)

    @pl.when(t == 0)
    def _():
        carry[...] = jnp.zeros(carry.shape, F32)

    f = lax.dot_general(wft_ref[...], n_ref[...], _NT_DIMS, preferred_element_type=F32)
    lf = jax.nn.log_sigmoid(f + bf_ref[...])
    lf_ref[...] = lf
    u = u_ref[...]
    c = carry[...]
    nh = lf.shape[0]
    for kk in range(tt // LANE):
        sl = slice(kk * LANE, (kk + 1) * LANE)
        cs = _dot_x_ones(lf[:, sl], u) + c
        c_ref[:, sl] = cs
        c = jnp.broadcast_to(cs[:, LANE - 1:LANE], (nh, LANE))
    carry[...] = c


def _logf_prompt(n, wft, b_f, bsz, seq, tt=512):
    m, d = n.shape
    nh = wft.shape[0]
    nt = seq // tt
    upper = (lax.broadcasted_iota(jnp.int32, (LANE, LANE), 0)
             <= lax.broadcasted_iota(jnp.int32, (LANE, LANE), 1)).astype(BF16)
    return pl.pallas_call(
        functools.partial(_logf_body, tt=tt),
        grid=(bsz, nt),
        in_specs=[pl.BlockSpec((nh, d), lambda b, t: (0, 0)),
                  pl.BlockSpec((tt, d), lambda b, t: (b * nt + t, 0)),
                  pl.BlockSpec((nh, 1), lambda b, t: (0, 0)),
                  pl.BlockSpec((LANE, LANE), lambda b, t: (0, 0))],
        out_specs=[pl.BlockSpec((nh, tt), lambda b, t: (0, b * nt + t)),
                   pl.BlockSpec((nh, tt), lambda b, t: (0, b * nt + t))],
        out_shape=[jax.ShapeDtypeStruct((nh, m), F32),
                   jax.ShapeDtypeStruct((nh, m), F32)],
        scratch_shapes=[pltpu.VMEM((nh, LANE), F32)],
        compiler_params=_params("parallel", "arbitrary"),
        name="logf_prompt",
    )(wft, n, b_f.reshape(nh, 1), upper)


def _flash_body(q_ref, k_ref, v_ref, g_ref, c_ref, o_ref, *, tq, nq, scale):
    c2 = c_ref[0] * LOG2E
    row = lax.broadcasted_iota(jnp.int32, (tq, tq), 0)
    col = lax.broadcasted_iota(jnp.int32, (tq, tq), 1)
    visible = col <= row

    def scores(q, lo, hi):
        s = lax.dot_general(q, k_ref[lo:hi, :], _NT_DIMS, preferred_element_type=F32)
        return s * (scale * LOG2E) - c2[:, lo:hi]

    for qi in range(nq):
        lo, hi = qi * tq, (qi + 1) * tq
        q = q_ref[lo:hi, :]
        s_diag = jnp.where(visible, scores(q, lo, hi), -jnp.inf)
        m = jnp.max(s_diag, axis=-1, keepdims=True)
        if qi > 0:
            s_past = scores(q, 0, lo)
            m = jnp.maximum(m, jnp.max(s_past, axis=-1, keepdims=True))
        p_diag = jnp.exp2(s_diag - m)
        l = jnp.sum(p_diag, axis=-1, keepdims=True)
        acc = jnp.dot(p_diag.astype(BF16), v_ref[lo:hi, :], preferred_element_type=F32)
        if qi > 0:
            p_past = jnp.exp2(s_past - m)
            l = l + jnp.sum(p_past, axis=-1, keepdims=True)
            acc = acc + jnp.dot(p_past.astype(BF16), v_ref[0:lo, :], preferred_element_type=F32)
        o_ref[lo:hi, :] = (acc / l * jax.nn.sigmoid(g_ref[lo:hi, :])).astype(o_ref.dtype)


def _fox_attend_prompt(q, k, v, g, c_rows, bsz, seq, tq=512):
    m, width = q.shape
    nh = width // HEAD_DIM
    tile = lambda: pl.BlockSpec((seq, HEAD_DIM), lambda b, h: (b, h))
    return pl.pallas_call(
        functools.partial(_flash_body, tq=tq, nq=seq // tq, scale=HEAD_DIM ** -0.5),
        grid=(bsz, nh),
        in_specs=[tile(), tile(), tile(), tile(),
                  pl.BlockSpec((1, 1, seq), lambda b, h: (b * nh + h, 0, 0))],
        out_specs=tile(),
        out_shape=jax.ShapeDtypeStruct((m, width), BF16),
        compiler_params=_params("parallel", "parallel"),
        name="fox_prompt_attn",
    )(q, k, v, g, c_rows)


def _page_cumsum_body(pt_ref, *refs, pages):
    lf_refs = refs[:pages]
    u_ref, o_ref, carry = refs[pages], refs[pages + 1], refs[pages + 2]
    j = pl.program_id(1)

    @pl.when(j == 0)
    def _():
        carry[...] = jnp.zeros(carry.shape, F32)

    u = u_ref[...]
    within_page = [_dot_x_ones(lf_refs[p][0, 0], u) for p in range(pages)]
    c = carry[...]
    for p in range(pages):
        cs = within_page[p] + c
        o_ref[0, p] = cs
        c = jnp.broadcast_to(cs[:, PAGE_SIZE - 1:PAGE_SIZE], c.shape)
    carry[...] = c


def _page_cumsum(cache_logf_t, layer, page_table, pages=16):
    dbs, n_pages = page_table.shape
    nh = cache_logf_t.shape[-2]
    upper = (lax.broadcasted_iota(jnp.int32, (PAGE_SIZE, PAGE_SIZE), 0)
             <= lax.broadcasted_iota(jnp.int32, (PAGE_SIZE, PAGE_SIZE), 1)).astype(BF16)

    def page_spec(p):
        return pl.BlockSpec((1, 1, nh, PAGE_SIZE),
                            lambda b, j, pt: (layer, pt[b, j * pages + p], 0, 0))

    grid_spec = pltpu.PrefetchScalarGridSpec(
        num_scalar_prefetch=1,
        grid=(dbs, n_pages // pages),
        in_specs=[page_spec(p) for p in range(pages)]
        + [pl.BlockSpec((PAGE_SIZE, PAGE_SIZE), lambda b, j, pt: (0, 0))],
        out_specs=pl.BlockSpec((1, pages, nh, PAGE_SIZE), lambda b, j, pt: (b, j, 0, 0)),
        scratch_shapes=[pltpu.VMEM((nh, PAGE_SIZE), F32)])
    return pl.pallas_call(
        functools.partial(_page_cumsum_body, pages=pages),
        grid_spec=grid_spec,
        out_shape=jax.ShapeDtypeStruct((dbs, n_pages, nh, PAGE_SIZE), F32),
        compiler_params=_params("parallel", "arbitrary"),
        name="page_cumsum",
    )(page_table, *([cache_logf_t] * pages), upper)


def _decode_body(pt_ref, *refs, pages, scale):
    q_ref = refs[0]
    k_refs = refs[1:1 + pages]
    v_refs = refs[1 + pages:1 + 2 * pages]
    (c_ref, hb_ref, kn_ref, vn_ref, lfn_ref, ctot_ref, g_ref,
     o_ref, m_scr, l_scr, acc_scr) = refs[1 + 2 * pages:]
    j = pl.program_id(1)

    @pl.when(j == 0)
    def _():
        m_scr[...] = jnp.full(m_scr.shape, -jnp.inf, F32)
        l_scr[...] = jnp.zeros(l_scr.shape, F32)
        acc_scr[...] = jnp.zeros(acc_scr.shape, F32)

    q = q_ref[0]
    hb = hb_ref[...]
    nh = q.shape[0]
    for p in range(pages):
        k = k_refs[p][0, 0].reshape(PAGE_SIZE * nh, HEAD_DIM).astype(BF16)
        v = v_refs[p][0, 0].reshape(PAGE_SIZE * nh, HEAD_DIM).astype(BF16)
        s = lax.dot_general(q, k, _NT_DIMS, preferred_element_type=F32) * scale
        s = s - c_ref[0, p] + hb
        m_prev = m_scr[...]
        m_new = jnp.maximum(m_prev, jnp.max(s, axis=-1, keepdims=True))
        alpha = jnp.exp(m_prev - m_new)
        pr = jnp.exp(s - m_new)
        l_scr[...] = alpha * l_scr[...] + jnp.sum(pr, axis=-1, keepdims=True)
        acc_scr[...] = alpha * acc_scr[...] + jnp.dot(pr.astype(BF16), v, preferred_element_type=F32)
        m_scr[...] = m_new

    @pl.when(j == pl.num_programs(1) - 1)
    def _():
        kn = kn_ref[0].astype(BF16).astype(F32)
        s_self = jnp.sum(q.astype(F32) * kn, axis=-1, keepdims=True) * scale
        s_self = s_self - (ctot_ref[0] + lfn_ref[0])
        m_prev = m_scr[...]
        m_new = jnp.maximum(m_prev, s_self)
        alpha = jnp.exp(m_prev - m_new)
        p_self = jnp.exp(s_self - m_new)
        l = alpha * l_scr[...] + p_self
        acc = alpha * acc_scr[...] + p_self * vn_ref[0]
        o_ref[0] = (acc / l) * jax.nn.sigmoid(g_ref[0])


def _fox_attend_sample(q, k_new, v_new, g, lf_new, c_tot, c_rows, head_bias,
                       cache_k, cache_v, layer, page_table, pages=4):
    dbs, n_pages = page_table.shape
    nh = q.shape[1]
    row_w = PAGE_SIZE * nh

    def page_spec(p):
        return pl.BlockSpec((1, 1, PAGE_SIZE, nh, HEAD_DIM),
                            lambda b, j, pt: (layer, pt[b, j * pages + p], 0, 0, 0))

    tok = lambda: pl.BlockSpec((1, nh, HEAD_DIM), lambda b, j, pt: (b, 0, 0))
    col = lambda: pl.BlockSpec((1, nh, 1), lambda b, j, pt: (b, 0, 0))
    grid_spec = pltpu.PrefetchScalarGridSpec(
        num_scalar_prefetch=1,
        grid=(dbs, n_pages // pages),
        in_specs=[tok()]
        + [page_spec(p) for p in range(pages)]
        + [page_spec(p) for p in range(pages)]
        + [pl.BlockSpec((1, pages, 1, row_w), lambda b, j, pt: (b, j, 0, 0)),
           pl.BlockSpec((nh, row_w), lambda b, j, pt: (0, 0)),
           tok(), tok(), col(), col(), tok()],
        out_specs=tok(),
        scratch_shapes=[pltpu.VMEM((nh, 1), F32), pltpu.VMEM((nh, 1), F32),
                        pltpu.VMEM((nh, HEAD_DIM), F32)])
    return pl.pallas_call(
        functools.partial(_decode_body, pages=pages, scale=HEAD_DIM ** -0.5),
        grid_spec=grid_spec,
        out_shape=jax.ShapeDtypeStruct((dbs, nh, HEAD_DIM), F32),
        compiler_params=_params("parallel", "arbitrary"),
        name="fox_sample_attn",
    )(page_table, q, *([cache_k] * pages), *([cache_v] * pages),
      c_rows, head_bias, k_new, v_new, lf_new, c_tot, g)


def _logf_step_body(n_ref, wf_ref, bf_ref, o_ref):
    f = jnp.dot(n_ref[...], wf_ref[...], preferred_element_type=F32)
    o_ref[...] = jax.nn.log_sigmoid(f + bf_ref[...])


def _logf_step(n, wf, b_f):
    rows = n.shape[0]
    nh = wf.shape[1]
    return pl.pallas_call(
        _logf_step_body,
        out_shape=jax.ShapeDtypeStruct((rows, nh), F32),
        name="logf_step",
    )(n, wf, b_f.reshape(1, nh))


def _ffn_half(xp, xs, g, w_gu, w_down, lead, dff):
    srows = xs.shape[0]
    hs = _ffn_up(_rmsnorm(xs, g), w_gu, lead, srows, 512)
    xs, w_down_b = _mm_residual(hs, w_down, lead, xs, 0.5, srows, 512, dff // 2, emit_w=True)
    hp = _ffn_up(_rmsnorm(xp, g), w_gu, lead, 2048, 256)
    xp = _mm_residual(hp, w_down_b, (), xp, 0.5, 1024, 256, dff)
    return xp, xs


def kernel(x_prompt, x_sample, state_conv, state_h, cache_k, cache_v, cache_logf, page_table,
           norm_g, w_ffn_gu, w_ffn_down, w_rec_in, conv_w, conv_b, w_gate_a, b_gate_a,
           w_gate_i, b_gate_i, lru_lambda, w_rec_out, w_fox_in, b_forget, q_norm_g, k_norm_g,
           w_fox_out):
    bsz, seq, d = x_prompt.shape
    dbs, dseq, _ = x_sample.shape
    assert dseq == 1, "sample group decodes one token per sequence"
    depth = norm_g.shape[0]
    nh = b_forget.shape[1]
    width = nh * HEAD_DIM
    dff = w_ffn_down.shape[2]
    n_pages = page_table.shape[1]
    srows = BF16_ROWS
    assert dbs <= srows

    w_ga_b = w_gate_a.astype(BF16)
    w_gi_b = w_gate_i.astype(BF16)

    xp = x_prompt.reshape(bsz * seq, d)
    xs = jnp.pad(x_sample.reshape(dbs, d), ((0, srows - dbs), (0, 0)))

    pm, pn = 2048, 256
    sn = 512
    w_fox_in_t = w_fox_in.transpose(0, 2, 1)
    cache_logf_t = cache_logf.transpose(0, 1, 3, 2)

    conv_p, h_p, k_p, v_p, lf_p = [], [], [], [], []
    conv_s, h_s, k_s, v_s, lf_s = [], [], [], [], []
    for i in range(depth):
        j = i // 2
        xp, xs = _ffn_half(xp, xs, norm_g[i, 0], w_ffn_gu, w_ffn_down, (i, 0), dff)
        up = _rmsnorm(xp, norm_g[i, 1])
        us = _rmsnorm(xs, norm_g[i, 1])
        if i % 2 == 0:
            lru_w = (w_ga_b[j], w_gi_b[j], b_gate_a[j], b_gate_i[j], lru_lambda[j], conv_w[j], conv_b[j])
            proj_p = _proj(up, w_rec_in, (j,), 0, 2 * d, pm, pn, [F32])
            gated_p, hl, cbuf = _lru_prompt(proj_p, bsz, seq, *lru_w)
            xp = _mm_residual(gated_p, w_rec_out, (j,), xp, 1.0, pm, pn, d, single_buffer_lhs=True)
            conv_p.append(cbuf)
            h_p.append(hl.reshape(bsz, d))

            proj_s = _proj(us, w_rec_in, (j,), 0, 2 * d, srows, sn, [F32])
            pad_rows = lambda a: jnp.pad(a, ((0, srows - dbs), (0, 0)))
            bufs = [pad_rows(state_conv[j, :, r, :]) for r in range(CONV_W - 1)]
            gated_s, h_new = _lru_step(proj_s, bufs, pad_rows(state_h[j]), *lru_w)
            xs = _mm_residual(gated_s, w_rec_out, (j,), xs, 1.0, srows, sn, d)
            conv_s.append(jnp.stack([state_conv[j, :, 1, :], state_conv[j, :, 2, :],
                                     proj_s[:dbs, d:]], axis=1))
            h_s.append(h_new[:dbs])
        else:
            w_f_t = w_fox_in_t[j, 4 * width:, :].astype(BF16)
            fox = functools.partial(_proj, w=w_fox_in_t, lead=(j,), ncols=width, w_rows_out=True)
            q = fox(up, col0=0, tm=pm, tn=pn, out_dtypes=[BF16], head_gain=q_norm_g[j])
            k, k_b = fox(up, col0=width, tm=pm, tn=pn, out_dtypes=[F32, BF16], head_gain=k_norm_g[j],
                         single_buffer_lhs=True)
            v, v_b = fox(up, col0=2 * width, tm=pm, tn=pn, out_dtypes=[F32, BF16],
                         single_buffer_lhs=True)
            g = fox(up, col0=3 * width, tm=pm, tn=pn, out_dtypes=[F32])
            lf_t, c_t = _logf_prompt(up, w_f_t, b_forget[j], bsz, seq)
            c_rows = c_t.reshape(nh, bsz, seq).transpose(1, 0, 2).reshape(bsz * nh, 1, seq)
            o = _fox_attend_prompt(q, k_b, v_b, g, c_rows, bsz, seq)
            xp = _mm_residual(o, w_fox_out, (j,), xp, 1.0, pm, pn, width, single_buffer_lhs=True)
            k_p.append(k.reshape(bsz, seq, nh, HEAD_DIM))
            v_p.append(v.reshape(bsz, seq, nh, HEAD_DIM))
            lf_p.append(lf_t.reshape(nh, bsz, seq).transpose(1, 2, 0))

            qs = fox(us, col0=0, tm=srows, tn=sn, out_dtypes=[BF16], head_gain=q_norm_g[j])
            ks = fox(us, col0=width, tm=srows, tn=sn, out_dtypes=[F32], head_gain=k_norm_g[j])
            vs = fox(us, col0=2 * width, tm=srows, tn=sn, out_dtypes=[F32])
            gs = fox(us, col0=3 * width, tm=srows, tn=sn, out_dtypes=[F32])
            lfs = _logf_step(us, w_f_t.T, b_forget[j])[:dbs]
            heads = lambda a: a[:dbs].reshape(dbs, nh, HEAD_DIM)
            c_past = _page_cumsum(cache_logf_t, j, page_table)
            c_tot = c_past[:, n_pages - 1, :, PAGE_SIZE - 1]
            c_rows_s = c_past.transpose(0, 1, 3, 2).reshape(dbs, n_pages, 1, PAGE_SIZE * nh)
            lane_head = lax.broadcasted_iota(jnp.int32, (nh, PAGE_SIZE * nh), 1) % nh
            row_head = lax.broadcasted_iota(jnp.int32, (nh, PAGE_SIZE * nh), 0)
            head_bias = jnp.where(lane_head == row_head, 0.0, -jnp.inf).astype(F32)
            os_ = _fox_attend_sample(heads(qs), heads(ks), heads(vs), heads(gs),
                                     lfs.reshape(dbs, nh, 1), c_tot.reshape(dbs, nh, 1),
                                     c_rows_s, head_bias, cache_k, cache_v, j, page_table)
            os_ = jnp.pad(os_.reshape(dbs, width), ((0, srows - dbs), (0, 0))).astype(BF16)
            xs = _mm_residual(os_, w_fox_out, (j,), xs, 1.0, srows, sn, width)
            k_s.append(ks[:dbs].reshape(dbs, 1, nh, HEAD_DIM))
            v_s.append(vs[:dbs].reshape(dbs, 1, nh, HEAD_DIM))
            lf_s.append(lfs.reshape(dbs, 1, nh))
        xp, xs = _ffn_half(xp, xs, norm_g[i, 2], w_ffn_gu, w_ffn_down, (i, 1), dff)

    return (xp.reshape(bsz, seq, d), xs[:dbs].reshape(dbs, 1, d),
            jnp.stack(conv_p), jnp.stack(h_p), jnp.stack(k_p), jnp.stack(v_p), jnp.stack(lf_p),
            jnp.stack(conv_s), jnp.stack(h_s), jnp.stack(k_s), jnp.stack(v_s), jnp.stack(lf_s))
```

```python
import functools

import jax
import jax.numpy as jnp
from jax import lax
from jax.experimental import pallas as pl
from jax.experimental.pallas import tpu as pltpu

F32 = jnp.float32
BF16 = jnp.bfloat16

EPS = 1e-6
LRU_C = 8.0
LRU_BLOCK = 256
CONV_W = 4
HEAD_DIM = 128
PAGE_SIZE = 128
LOG2E = 1.4426950408889634

LANE = 128
SUBLANE = 8
BF16_ROWS = 16
VMEM_BYTES_V7X = 64 * 1024 * 1024
VMEM_LIMIT = VMEM_BYTES_V7X - 6 * 1024 * 1024

_NT_DIMS = (((1,), (1,)), ((), ()))


def _params(*semantics):
    return pltpu.CompilerParams(dimension_semantics=semantics, vmem_limit_bytes=VMEM_LIMIT)


def _weight_spec(block, lead, index_map):
    nlead = len(lead)
    return pl.BlockSpec((None,) * nlead + tuple(block),
                        lambda *g: tuple(lead) + tuple(index_map(*g)))


def _rms_body(x_ref, g_ref, o_ref):
    x = x_ref[...]
    ms = jnp.mean(x * x, axis=-1, keepdims=True)
    o_ref[...] = (x * lax.rsqrt(ms + EPS) * g_ref[...]).astype(o_ref.dtype)


def _rmsnorm(x, g):
    m, d = x.shape
    tr = min(m, 256)
    return pl.pallas_call(
        _rms_body,
        grid=(m // tr,),
        in_specs=[pl.BlockSpec((tr, d), lambda i: (i, 0)),
                  pl.BlockSpec((1, d), lambda i: (0, 0))],
        out_specs=pl.BlockSpec((tr, d), lambda i: (i, 0)),
        out_shape=jax.ShapeDtypeStruct((m, d), BF16),
        compiler_params=_params("parallel"),
        name="rmsnorm",
    )(x, g.reshape(1, d))


def _ride(out_refs, compute):
    first = pl.program_id(0) == 0

    @pl.when(first)
    def _():
        compute()

    @pl.when(jnp.logical_not(first))
    def _():
        for o_ref in out_refs:
            o_ref[...] = jnp.zeros(o_ref.shape, o_ref.dtype)


def _rider_out(n_tiles, rows, cols, tn, dtype, nd):
    if nd == 2:
        spec = pl.BlockSpec((rows, tn), lambda i, j: (i, j))
    else:
        spec = pl.BlockSpec((rows, tn), lambda i, j, k: (i, j))
    return spec, jax.ShapeDtypeStruct((n_tiles * rows, cols), dtype)


def _swiglu_body(a_ref, wg_ref, wu_ref, *rest, rider):
    wg = wg_ref[...].astype(BF16)
    wu = wu_ref[...].astype(BF16)

    def swiglu(a, o_ref):
        g = jnp.dot(a, wg, preferred_element_type=F32)
        u = jnp.dot(a, wu, preferred_element_type=F32)
        o_ref[...] = (jax.nn.silu(g) * u).astype(o_ref.dtype)

    if rider:
        as_ref, o_ref, os_ref = rest
        _ride([os_ref], lambda: swiglu(as_ref[...], os_ref))
    else:
        o_ref, = rest
    swiglu(a_ref[...], o_ref)


def _ffn_up(n, w_gu, lead, tm, tn, rider=None):
    m, d = n.shape
    dff = w_gu.shape[-1] // 2
    nj = dff // tn
    in_specs = [pl.BlockSpec((tm, d), lambda i, j: (i, 0)),
                _weight_spec((d, tn), lead, lambda i, j: (0, j)),
                _weight_spec((d, tn), lead, lambda i, j: (0, j + nj))]
    out_specs = [pl.BlockSpec((tm, tn), lambda i, j: (i, j))]
    out_shape = [jax.ShapeDtypeStruct((m, dff), BF16)]
    args = [n, w_gu, w_gu]
    if rider is not None:
        rows = rider.shape[0]
        in_specs.append(pl.BlockSpec((rows, d), lambda i, j: (0, 0)))
        args.append(rider)
        spec, shape = _rider_out(m // tm, rows, dff, tn, BF16, 2)
        out_specs.append(spec)
        out_shape.append(shape)
    outs = pl.pallas_call(
        functools.partial(_swiglu_body, rider=rider is not None),
        grid=(m // tm, nj),
        in_specs=in_specs,
        out_specs=out_specs,
        out_shape=out_shape,
        compiler_params=_params("parallel", "parallel"),
        name="ffn_up",
    )(*args)
    if rider is None:
        return outs[0]
    return outs[0], outs[1][:rider.shape[0]]


def _mm_res_body(a_ref, w_ref, r_ref, *rest, nk, scale, emit_w, rider):
    if rider:
        as_ref, rs_ref = rest[:2]
        rest = rest[2:]
    o_ref, rest = rest[0], rest[1:]
    w = w_ref[...].astype(BF16)
    if emit_w:
        rest[0][...] = w
        rest = rest[1:]
    if rider:
        os_ref, rest = rest[0], rest[1:]

        def sample_rows():
            os_ref[...] = rs_ref[...] + scale * jnp.dot(as_ref[...], w, preferred_element_type=F32)

        _ride([os_ref], sample_rows)
    part = jnp.dot(a_ref[...], w, preferred_element_type=F32)
    if nk == 1:
        o_ref[...] = r_ref[...] + scale * part
        return
    acc_ref, = rest
    k = pl.program_id(2)

    @pl.when(k == 0)
    def _():
        acc_ref[...] = part

    @pl.when(jnp.logical_and(k > 0, k < nk - 1))
    def _():
        acc_ref[...] += part

    @pl.when(k == nk - 1)
    def _():
        o_ref[...] = r_ref[...] + scale * (acc_ref[...] + part)


def _mm_residual(a, w, lead, res, scale, tm, tn, tk, emit_w=False, single_buffer_lhs=False,
                 rider=None):
    m, kdim = a.shape
    n = w.shape[-1]
    nk = kdim // tk
    assert not emit_w or m == tm
    assert rider is None or nk == 1
    scratch = [pltpu.VMEM((tm, tn), F32)] if nk > 1 else []
    lhs_mode = pl.Buffered(1) if single_buffer_lhs else None
    in_specs = [pl.BlockSpec((tm, tk), lambda i, j, k: (i, k), pipeline_mode=lhs_mode),
                _weight_spec((tk, tn), lead, lambda i, j, k: (k, j)),
                pl.BlockSpec((tm, tn), lambda i, j, k: (i, j))]
    args = [a, w, res]
    out_specs = [pl.BlockSpec((tm, tn), lambda i, j, k: (i, j))]
    out_shape = [jax.ShapeDtypeStruct((m, n), F32)]
    if rider is not None:
        rows = rider[0].shape[0]
        in_specs += [pl.BlockSpec((rows, kdim), lambda i, j, k: (0, 0)),
                     pl.BlockSpec((rows, tn), lambda i, j, k: (0, j))]
        args += list(rider)
    if emit_w:
        out_specs.append(pl.BlockSpec((tk, tn), lambda i, j, k: (k, j)))
        out_shape.append(jax.ShapeDtypeStruct((kdim, n), BF16))
    if rider is not None:
        spec, shape = _rider_out(m // tm, rows, n, tn, F32, 3)
        out_specs.append(spec)
        out_shape.append(shape)
    outs = list(pl.pallas_call(
        functools.partial(_mm_res_body, nk=nk, scale=scale, emit_w=emit_w,
                          rider=rider is not None),
        grid=(m // tm, n // tn, nk),
        in_specs=in_specs,
        out_specs=out_specs,
        out_shape=out_shape,
        scratch_shapes=scratch,
        compiler_params=_params("parallel", "parallel", "arbitrary"),
        name="mm_residual",
    )(*args))
    if rider is not None:
        outs[-1] = outs[-1][:rows]
    return outs if len(outs) > 1 else outs[0]


def _proj_body(a_ref, w_ref, *rest, tn, head_norm, w_rows_out, n_out, rider):
    w = w_ref[...].astype(BF16)
    g = None
    if head_norm:
        g, rest = rest[0][...], rest[1:]
    if rider:
        as_ref, rest = rest[0], rest[1:]
    out_refs, rider_refs = rest[:n_out], rest[n_out:]

    def project(a, refs):
        if w_rows_out:
            acc = lax.dot_general(a, w, _NT_DIMS, preferred_element_type=F32)
        else:
            acc = jnp.dot(a, w, preferred_element_type=F32)
        if not head_norm:
            for o_ref in refs:
                o_ref[...] = acc.astype(o_ref.dtype)
            return
        for hh in range(tn // HEAD_DIM):
            sl = slice(hh * HEAD_DIM, (hh + 1) * HEAD_DIM)
            blk = acc[:, sl]
            ms = jnp.mean(blk * blk, axis=-1, keepdims=True)
            y = blk * lax.rsqrt(ms + EPS) * g
            for o_ref in refs:
                o_ref[:, sl] = y.astype(o_ref.dtype)

    if rider:
        _ride(rider_refs, lambda: project(as_ref[...], rider_refs))
    project(a_ref[...], out_refs)


def _proj(a, w, lead, col0, ncols, tm, tn, out_dtypes, head_gain=None, w_rows_out=False,
          single_buffer_lhs=False, rider=None, rider_dtypes=()):
    m, kdim = a.shape
    joff = col0 // tn
    if w_rows_out:
        w_spec = _weight_spec((tn, kdim), lead, lambda i, j: (j + joff, 0))
    else:
        w_spec = _weight_spec((kdim, tn), lead, lambda i, j: (0, j + joff))
    lhs_mode = pl.Buffered(1) if single_buffer_lhs else None
    in_specs = [pl.BlockSpec((tm, kdim), lambda i, j: (i, 0), pipeline_mode=lhs_mode), w_spec]
    args = [a, w]
    if head_gain is not None:
        in_specs.append(pl.BlockSpec((1, HEAD_DIM), lambda i, j: (0, 0)))
        args.append(head_gain.reshape(1, HEAD_DIM))
    out_specs = [pl.BlockSpec((tm, tn), lambda i, j: (i, j)) for _ in out_dtypes]
    out_shape = [jax.ShapeDtypeStruct((m, ncols), dt) for dt in out_dtypes]
    if rider is not None:
        rows = rider.shape[0]
        in_specs.append(pl.BlockSpec((rows, kdim), lambda i, j: (0, 0)))
        args.append(rider)
        for dt in rider_dtypes:
            spec, shape = _rider_out(m // tm, rows, ncols, tn, dt, 2)
            out_specs.append(spec)
            out_shape.append(shape)
    outs = list(pl.pallas_call(
        functools.partial(_proj_body, tn=tn, head_norm=head_gain is not None,
                          w_rows_out=w_rows_out, n_out=len(out_dtypes), rider=rider is not None),
        grid=(m // tm, ncols // tn),
        in_specs=in_specs,
        out_specs=out_specs,
        out_shape=out_shape,
        compiler_params=_params("parallel", "parallel"),
        name="proj",
    )(*args))
    if rider is not None:
        for t in range(len(out_dtypes), len(outs)):
            outs[t] = outs[t][:rows]
    return outs


def _lru_gates(xc, wa_ref, wi_ref, ba, bi, lam, nblk):
    xcb = xc.astype(BF16)
    a_parts, b_parts = [], []
    for blk in range(nblk):
        sl = slice(blk * LRU_BLOCK, (blk + 1) * LRU_BLOCK)
        gate_r = jnp.dot(xcb[:, sl], wa_ref[blk], preferred_element_type=F32) + ba[:, sl]
        gate_i = jnp.dot(xcb[:, sl], wi_ref[blk], preferred_element_type=F32) + bi[:, sl]
        r = jax.nn.sigmoid(gate_r)
        ig = jax.nn.sigmoid(gate_i)
        log_a = -LRU_C * r * jax.nn.softplus(-lam[:, sl])
        a_parts.append(jnp.exp(log_a))
        th = jnp.tanh(log_a)
        b_parts.append(jnp.sqrt(-2.0 * th / (1.0 - th)) * (ig * xc[:, sl]))
    return a_parts, b_parts


def _lru_scan_body(xr_ref, y_ref, wa_ref, wi_ref, ba_ref, bi_ref, lam_ref, cw_ref, cb_ref,
                   o_ref, hl_ref, cbuf_ref, xpad, a_scr, b_scr, h_scr, *, tt, cw_cols):
    t = pl.program_id(2)
    nblk = cw_cols // LRU_BLOCK

    @pl.when(t == 0)
    def _():
        xpad[0:SUBLANE, :] = jnp.zeros((SUBLANE, cw_cols), F32)
        h_scr[...] = jnp.zeros((SUBLANE, cw_cols), F32)

    xr = xr_ref[...]
    xpad[SUBLANE:SUBLANE + tt, :] = xr
    cw = cw_ref[...]
    xc = cb_ref[...] + cw[0:1] * xpad[SUBLANE - 3:SUBLANE - 3 + tt, :]
    xc = xc + cw[1:2] * xpad[SUBLANE - 2:SUBLANE - 2 + tt, :]
    xc = xc + cw[2:3] * xpad[SUBLANE - 1:SUBLANE - 1 + tt, :]
    xc = xc + cw[3:4] * xr
    a_parts, b_parts = _lru_gates(xc, wa_ref, wi_ref, ba_ref[...], bi_ref[...], lam_ref[...], nblk)
    for blk in range(nblk):
        sl = slice(blk * LRU_BLOCK, (blk + 1) * LRU_BLOCK)
        a_scr[:, sl] = a_parts[blk]
        b_scr[:, sl] = b_parts[blk]

    row = lax.broadcasted_iota(jnp.int32, (SUBLANE, cw_cols), 0)

    def group(gi, h):
        off = pl.multiple_of(gi * SUBLANE, SUBLANE)
        a = a_scr[pl.ds(off, SUBLANE), :]
        b = b_scr[pl.ds(off, SUBLANE), :]
        for s in (1, 2, 4):
            a_sh = jnp.where(row >= s, pltpu.roll(a, s, 0), 1.0)
            b_sh = jnp.where(row >= s, pltpu.roll(b, s, 0), 0.0)
            b = a * b_sh + b
            a = a * a_sh
        hcur = a * h + b
        a_scr[pl.ds(off, SUBLANE), :] = hcur
        return jnp.broadcast_to(hcur[SUBLANE - 1:SUBLANE, :], (SUBLANE, cw_cols))

    h = lax.fori_loop(0, tt // SUBLANE, group, h_scr[...], unroll=4)
    h_scr[...] = h
    xpad[0:SUBLANE, :] = xpad[tt:tt + SUBLANE, :]
    o_ref[...] = (jax.nn.gelu(y_ref[...]) * a_scr[...]).astype(o_ref.dtype)

    @pl.when(t == pl.num_programs(2) - 1)
    def _():
        hl_ref[0] = h[0:1, :]
        cbuf_ref[0] = xr_ref[tt - (CONV_W - 1):tt, :]


def _lru_prompt(proj, bsz, seq, wa, wi, ba, bi, lam, cw, cb, tt=512, cw_cols=1024):
    d = proj.shape[1] // 2
    nc = d // cw_cols
    nt = seq // tt
    gpb = cw_cols // LRU_BLOCK
    vec = lambda: pl.BlockSpec((1, cw_cols), lambda b, c, t: (0, c))
    return pl.pallas_call(
        functools.partial(_lru_scan_body, tt=tt, cw_cols=cw_cols),
        grid=(bsz, nc, nt),
        in_specs=[pl.BlockSpec((tt, cw_cols), lambda b, c, t: (b * nt + t, nc + c)),
                  pl.BlockSpec((tt, cw_cols), lambda b, c, t: (b * nt + t, c)),
                  pl.BlockSpec((gpb, LRU_BLOCK, LRU_BLOCK), lambda b, c, t: (c, 0, 0)),
                  pl.BlockSpec((gpb, LRU_BLOCK, LRU_BLOCK), lambda b, c, t: (c, 0, 0)),
                  vec(), vec(), vec(),
                  pl.BlockSpec((CONV_W, cw_cols), lambda b, c, t: (0, c)),
                  vec()],
        out_specs=[pl.BlockSpec((tt, cw_cols), lambda b, c, t: (b * nt + t, c)),
                   pl.BlockSpec((1, 1, cw_cols), lambda b, c, t: (b, 0, c)),
                   pl.BlockSpec((1, CONV_W - 1, cw_cols), lambda b, c, t: (b, 0, c))],
        out_shape=[jax.ShapeDtypeStruct((bsz * seq, d), BF16),
                   jax.ShapeDtypeStruct((bsz, 1, d), F32),
                   jax.ShapeDtypeStruct((bsz, CONV_W - 1, d), F32)],
        scratch_shapes=[pltpu.VMEM((tt + SUBLANE, cw_cols), F32),
                        pltpu.VMEM((tt, cw_cols), F32),
                        pltpu.VMEM((tt, cw_cols), F32),
                        pltpu.VMEM((SUBLANE, cw_cols), F32)],
        compiler_params=_params("parallel", "parallel", "arbitrary"),
        name="lru_scan",
    )(proj, proj, wa, wi, ba.reshape(1, d), bi.reshape(1, d), lam.reshape(1, d), cw, cb.reshape(1, d))


def _lru_step_body(xr_ref, y_ref, b0_ref, b1_ref, b2_ref, h0_ref, wa_ref, wi_ref, ba_ref, bi_ref,
                   lam_ref, cw_ref, cb_ref, o_ref, h_ref, *, cw_cols):
    nblk = cw_cols // LRU_BLOCK
    cw = cw_ref[...]
    xr = xr_ref[...]
    xc = cb_ref[...] + cw[0:1] * b0_ref[...]
    xc = xc + cw[1:2] * b1_ref[...]
    xc = xc + cw[2:3] * b2_ref[...]
    xc = xc + cw[3:4] * xr
    a_parts, b_parts = _lru_gates(xc, wa_ref, wi_ref, ba_ref[...], bi_ref[...], lam_ref[...], nblk)
    y = y_ref[...]
    for blk in range(nblk):
        sl = slice(blk * LRU_BLOCK, (blk + 1) * LRU_BLOCK)
        h = a_parts[blk] * h0_ref[:, sl] + b_parts[blk]
        h_ref[:, sl] = h
        o_ref[:, sl] = (jax.nn.gelu(y[:, sl]) * h).astype(o_ref.dtype)


def _lru_step(proj, bufs, h0, wa, wi, ba, bi, lam, cw, cb, cw_cols=1024):
    rows = proj.shape[0]
    d = proj.shape[1] // 2
    nc = d // cw_cols
    gpb = cw_cols // LRU_BLOCK
    act = lambda off: pl.BlockSpec((rows, cw_cols), lambda c: (0, c + off))
    vec = lambda: pl.BlockSpec((1, cw_cols), lambda c: (0, c))
    return pl.pallas_call(
        functools.partial(_lru_step_body, cw_cols=cw_cols),
        grid=(nc,),
        in_specs=[act(nc), act(0), act(0), act(0), act(0), act(0),
                  pl.BlockSpec((gpb, LRU_BLOCK, LRU_BLOCK), lambda c: (c, 0, 0)),
                  pl.BlockSpec((gpb, LRU_BLOCK, LRU_BLOCK), lambda c: (c, 0, 0)),
                  vec(), vec(), vec(),
                  pl.BlockSpec((CONV_W, cw_cols), lambda c: (0, c)),
                  vec()],
        out_specs=[act(0), act(0)],
        out_shape=[jax.ShapeDtypeStruct((rows, d), BF16),
                   jax.ShapeDtypeStruct((rows, d), F32)],
        compiler_params=_params("parallel"),
        name="lru_step",
    )(proj, proj, bufs[0], bufs[1], bufs[2], h0, wa, wi,
      ba.reshape(1, d), bi.reshape(1, d), lam.reshape(1, d), cw, cb.reshape(1, d))


def _split3(x):
    hi = x.astype(BF16)
    r1 = x - hi.astype(F32)
    mid = r1.astype(BF16)
    lo = (r1 - mid.astype(F32)).astype(BF16)
    return hi, mid, lo


def _dot_x_ones(x, ones_rhs):
    hi, mid, lo = _split3(x)
    d = lambda p: jnp.dot(p, ones_rhs, preferred_element_type=F32)
    return d(hi) + d(mid) + d(lo)


def _logf_body(wft_ref, n_ref, bf_ref, u_ref, lf_ref, c_ref, carry, *, tt):
    t = pl.program_id(1)

    @pl.when(t == 0)
    def _():
        carry[...] = jnp.zeros(carry.shape, F32)

    f = lax.dot_general(wft_ref[...], n_ref[...], _NT_DIMS, preferred_element_type=F32)
    lf = jax.nn.log_sigmoid(f + bf_ref[...])
    lf_ref[...] = lf
    u = u_ref[...]
    c = carry[...]
    nh = lf.shape[0]
    for kk in range(tt // LANE):
        sl = slice(kk * LANE, (kk + 1) * LANE)
        cs = _dot_x_ones(lf[:, sl], u) + c
        c_ref[:, sl] = cs
        c = jnp.broadcast_to(cs[:, LANE - 1:LANE], (nh, LANE))
    carry[...] = c


def _logf_prompt(n, wft, b_f, bsz, seq, tt=512):
    m, d = n.shape
    nh = wft.shape[0]
    nt = seq // tt
    upper = (lax.broadcasted_iota(jnp.int32, (LANE, LANE), 0)
             <= lax.broadcasted_iota(jnp.int32, (LANE, LANE), 1)).astype(BF16)
    return pl.pallas_call(
        functools.partial(_logf_body, tt=tt),
        grid=(bsz, nt),
        in_specs=[pl.BlockSpec((nh, d), lambda b, t: (0, 0)),
                  pl.BlockSpec((tt, d), lambda b, t: (b * nt + t, 0)),
                  pl.BlockSpec((nh, 1), lambda b, t: (0, 0)),
                  pl.BlockSpec((LANE, LANE), lambda b, t: (0, 0))],
        out_specs=[pl.BlockSpec((nh, tt), lambda b, t: (0, b * nt + t)),
                   pl.BlockSpec((nh, tt), lambda b, t: (0, b * nt + t))],
        out_shape=[jax.ShapeDtypeStruct((nh, m), F32),
                   jax.ShapeDtypeStruct((nh, m), F32)],
        scratch_shapes=[pltpu.VMEM((nh, LANE), F32)],
        compiler_params=_params("parallel", "arbitrary"),
        name="logf_prompt",
    )(wft, n, b_f.reshape(nh, 1), upper)


def _flash_body(q_ref, k_ref, v_ref, g_ref, c_ref, o_ref, *, tq, nq, scale):
    c2 = c_ref[0] * LOG2E
    row = lax.broadcasted_iota(jnp.int32, (tq, tq), 0)
    col = lax.broadcasted_iota(jnp.int32, (tq, tq), 1)
    visible = col <= row

    def scores(q, lo, hi):
        s = lax.dot_general(q, k_ref[lo:hi, :], _NT_DIMS, preferred_element_type=F32)
        return s * (scale * LOG2E) - c2[:, lo:hi]

    for qi in range(nq):
        lo, hi = qi * tq, (qi + 1) * tq
        q = q_ref[lo:hi, :]
        s_diag = jnp.where(visible, scores(q, lo, hi), -jnp.inf)
        m = jnp.max(s_diag, axis=-1, keepdims=True)
        if qi > 0:
            s_past = scores(q, 0, lo)
            m = jnp.maximum(m, jnp.max(s_past, axis=-1, keepdims=True))
        p_diag = jnp.exp2(s_diag - m)
        l = jnp.sum(p_diag, axis=-1, keepdims=True)
        acc = jnp.dot(p_diag.astype(BF16), v_ref[lo:hi, :], preferred_element_type=F32)
        if qi > 0:
            p_past = jnp.exp2(s_past - m)
            l = l + jnp.sum(p_past, axis=-1, keepdims=True)
            acc = acc + jnp.dot(p_past.astype(BF16), v_ref[0:lo, :], preferred_element_type=F32)
        o_ref[lo:hi, :] = (acc / l * jax.nn.sigmoid(g_ref[lo:hi, :])).astype(o_ref.dtype)


def _fox_attend_prompt(q, k, v, g, c_rows, bsz, seq, tq=512):
    m, width = q.shape
    nh = width // HEAD_DIM
    tile = lambda: pl.BlockSpec((seq, HEAD_DIM), lambda b, h: (b, h))
    return pl.pallas_call(
        functools.partial(_flash_body, tq=tq, nq=seq // tq, scale=HEAD_DIM ** -0.5),
        grid=(bsz, nh),
        in_specs=[tile(), tile(), tile(), tile(),
                  pl.BlockSpec((1, 1, seq), lambda b, h: (b * nh + h, 0, 0))],
        out_specs=tile(),
        out_shape=jax.ShapeDtypeStruct((m, width), BF16),
        compiler_params=_params("parallel", "parallel"),
        name="fox_prompt_attn",
    )(q, k, v, g, c_rows)


def _page_cumsum_body(pt_ref, *refs, pages):
    lf_refs = refs[:pages]
    u_ref, o_ref, carry = refs[pages], refs[pages + 1], refs[pages + 2]
    j = pl.program_id(1)

    @pl.when(j == 0)
    def _():
        carry[...] = jnp.zeros(carry.shape, F32)

    u = u_ref[...]
    within_page = [_dot_x_ones(lf_refs[p][0, 0], u) for p in range(pages)]
    c = carry[...]
    for p in range(pages):
        cs = within_page[p] + c
        o_ref[0, p] = cs
        c = jnp.broadcast_to(cs[:, PAGE_SIZE - 1:PAGE_SIZE], c.shape)
    carry[...] = c


def _page_cumsum(cache_logf_t, layer, page_table, pages=16):
    dbs, n_pages = page_table.shape
    nh = cache_logf_t.shape[-2]
    upper = (lax.broadcasted_iota(jnp.int32, (PAGE_SIZE, PAGE_SIZE), 0)
             <= lax.broadcasted_iota(jnp.int32, (PAGE_SIZE, PAGE_SIZE), 1)).astype(BF16)

    def page_spec(p):
        return pl.BlockSpec((1, 1, nh, PAGE_SIZE),
                            lambda b, j, pt: (layer, pt[b, j * pages + p], 0, 0))

    grid_spec = pltpu.PrefetchScalarGridSpec(
        num_scalar_prefetch=1,
        grid=(dbs, n_pages // pages),
        in_specs=[page_spec(p) for p in range(pages)]
        + [pl.BlockSpec((PAGE_SIZE, PAGE_SIZE), lambda b, j, pt: (0, 0))],
        out_specs=pl.BlockSpec((1, pages, nh, PAGE_SIZE), lambda b, j, pt: (b, j, 0, 0)),
        scratch_shapes=[pltpu.VMEM((nh, PAGE_SIZE), F32)])
    return pl.pallas_call(
        functools.partial(_page_cumsum_body, pages=pages),
        grid_spec=grid_spec,
        out_shape=jax.ShapeDtypeStruct((dbs, n_pages, nh, PAGE_SIZE), F32),
        compiler_params=_params("parallel", "arbitrary"),
        name="page_cumsum",
    )(page_table, *([cache_logf_t] * pages), upper)


def _decode_body(pt_ref, *refs, pages, scale):
    q_ref = refs[0]
    k_refs = refs[1:1 + pages]
    v_refs = refs[1 + pages:1 + 2 * pages]
    (c_ref, hb_ref, kn_ref, vn_ref, lfn_ref, ctot_ref, g_ref,
     o_ref, m_scr, l_scr, acc_scr) = refs[1 + 2 * pages:]
    j = pl.program_id(1)

    @pl.when(j == 0)
    def _():
        m_scr[...] = jnp.full(m_scr.shape, -jnp.inf, F32)
        l_scr[...] = jnp.zeros(l_scr.shape, F32)
        acc_scr[...] = jnp.zeros(acc_scr.shape, F32)

    q = q_ref[0]
    hb = hb_ref[...]
    nh = q.shape[0]
    for p in range(pages):
        k = k_refs[p][0, 0].reshape(PAGE_SIZE * nh, HEAD_DIM).astype(BF16)
        v = v_refs[p][0, 0].reshape(PAGE_SIZE * nh, HEAD_DIM).astype(BF16)
        s = lax.dot_general(q, k, _NT_DIMS, preferred_element_type=F32) * scale
        s = s - c_ref[0, p] + hb
        m_prev = m_scr[...]
        m_new = jnp.maximum(m_prev, jnp.max(s, axis=-1, keepdims=True))
        alpha = jnp.exp(m_prev - m_new)
        pr = jnp.exp(s - m_new)
        l_scr[...] = alpha * l_scr[...] + jnp.sum(pr, axis=-1, keepdims=True)
        acc_scr[...] = alpha * acc_scr[...] + jnp.dot(pr.astype(BF16), v, preferred_element_type=F32)
        m_scr[...] = m_new

    @pl.when(j == pl.num_programs(1) - 1)
    def _():
        kn = kn_ref[0].astype(BF16).astype(F32)
        s_self = jnp.sum(q.astype(F32) * kn, axis=-1, keepdims=True) * scale
        s_self = s_self - (ctot_ref[0] + lfn_ref[0])
        m_prev = m_scr[...]
        m_new = jnp.maximum(m_prev, s_self)
        alpha = jnp.exp(m_prev - m_new)
        p_self = jnp.exp(s_self - m_new)
        l = alpha * l_scr[...] + p_self
        acc = alpha * acc_scr[...] + p_self * vn_ref[0]
        o_ref[0] = (acc / l) * jax.nn.sigmoid(g_ref[0])


def _fox_attend_sample(q, k_new, v_new, g, lf_new, c_tot, c_rows, head_bias,
                       cache_k, cache_v, layer, page_table, pages=4):
    dbs, n_pages = page_table.shape
    nh = q.shape[1]
    row_w = PAGE_SIZE * nh

    def page_spec(p):
        return pl.BlockSpec((1, 1, PAGE_SIZE, nh, HEAD_DIM),
                            lambda b, j, pt: (layer, pt[b, j * pages + p], 0, 0, 0))

    tok = lambda: pl.BlockSpec((1, nh, HEAD_DIM), lambda b, j, pt: (b, 0, 0))
    col = lambda: pl.BlockSpec((1, nh, 1), lambda b, j, pt: (b, 0, 0))
    grid_spec = pltpu.PrefetchScalarGridSpec(
        num_scalar_prefetch=1,
        grid=(dbs, n_pages // pages),
        in_specs=[tok()]
        + [page_spec(p) for p in range(pages)]
        + [page_spec(p) for p in range(pages)]
        + [pl.BlockSpec((1, pages, 1, row_w), lambda b, j, pt: (b, j, 0, 0)),
           pl.BlockSpec((nh, row_w), lambda b, j, pt: (0, 0)),
           tok(), tok(), col(), col(), tok()],
        out_specs=tok(),
        scratch_shapes=[pltpu.VMEM((nh, 1), F32), pltpu.VMEM((nh, 1), F32),
                        pltpu.VMEM((nh, HEAD_DIM), F32)])
    return pl.pallas_call(
        functools.partial(_decode_body, pages=pages, scale=HEAD_DIM ** -0.5),
        grid_spec=grid_spec,
        out_shape=jax.ShapeDtypeStruct((dbs, nh, HEAD_DIM), F32),
        compiler_params=_params("parallel", "arbitrary"),
        name="fox_sample_attn",
    )(page_table, q, *([cache_k] * pages), *([cache_v] * pages),
      c_rows, head_bias, k_new, v_new, lf_new, c_tot, g)


def _logf_step_body(n_ref, wf_ref, bf_ref, o_ref):
    f = jnp.dot(n_ref[...], wf_ref[...], preferred_element_type=F32)
    o_ref[...] = jax.nn.log_sigmoid(f + bf_ref[...])


def _logf_step(n, wf, b_f):
    rows = n.shape[0]
    nh = wf.shape[1]
    return pl.pallas_call(
        _logf_step_body,
        out_shape=jax.ShapeDtypeStruct((rows, nh), F32),
        name="logf_step",
    )(n, wf, b_f.reshape(1, nh))


def _ffn_half(xp, xs, g, w_gu, w_down, lead, dff):
    srows = xs.shape[0]
    hp, hs = _ffn_up(_rmsnorm(xp, g), w_gu, lead, 2048, 256, rider=_rmsnorm(xs, g))
    xs, w_down_b = _mm_residual(hs, w_down, lead, xs, 0.5, srows, 512, dff // 2, emit_w=True)
    xp = _mm_residual(hp, w_down_b, (), xp, 0.5, 1024, 256, dff)
    return xp, xs


def kernel(x_prompt, x_sample, state_conv, state_h, cache_k, cache_v, cache_logf, page_table,
           norm_g, w_ffn_gu, w_ffn_down, w_rec_in, conv_w, conv_b, w_gate_a, b_gate_a,
           w_gate_i, b_gate_i, lru_lambda, w_rec_out, w_fox_in, b_forget, q_norm_g, k_norm_g,
           w_fox_out):
    bsz, seq, d = x_prompt.shape
    dbs, dseq, _ = x_sample.shape
    assert dseq == 1, "sample group decodes one token per sequence"
    depth = norm_g.shape[0]
    nh = b_forget.shape[1]
    width = nh * HEAD_DIM
    dff = w_ffn_down.shape[2]
    n_pages = page_table.shape[1]
    srows = BF16_ROWS
    assert dbs <= srows

    w_ga_b = w_gate_a.astype(BF16)
    w_gi_b = w_gate_i.astype(BF16)

    xp = x_prompt.reshape(bsz * seq, d)
    xs = jnp.pad(x_sample.reshape(dbs, d), ((0, srows - dbs), (0, 0)))

    pm, pn = 2048, 256
    w_fox_in_t = w_fox_in.transpose(0, 2, 1)
    cache_logf_t = cache_logf.transpose(0, 1, 3, 2)

    conv_p, h_p, k_p, v_p, lf_p = [], [], [], [], []
    conv_s, h_s, k_s, v_s, lf_s = [], [], [], [], []
    for i in range(depth):
        j = i // 2
        xp, xs = _ffn_half(xp, xs, norm_g[i, 0], w_ffn_gu, w_ffn_down, (i, 0), dff)
        up = _rmsnorm(xp, norm_g[i, 1])
        us = _rmsnorm(xs, norm_g[i, 1])
        if i % 2 == 0:
            lru_w = (w_ga_b[j], w_gi_b[j], b_gate_a[j], b_gate_i[j], lru_lambda[j], conv_w[j], conv_b[j])
            proj_p, proj_s = _proj(up, w_rec_in, (j,), 0, 2 * d, pm, pn, [F32],
                                   rider=us, rider_dtypes=[F32])
            gated_p, hl, cbuf = _lru_prompt(proj_p, bsz, seq, *lru_w)
            pad_rows = lambda a: jnp.pad(a, ((0, srows - dbs), (0, 0)))
            bufs = [pad_rows(state_conv[j, :, r, :]) for r in range(CONV_W - 1)]
            gated_s, h_new = _lru_step(proj_s, bufs, pad_rows(state_h[j]), *lru_w)
            xp, xs = _mm_residual(gated_p, w_rec_out, (j,), xp, 1.0, pm, pn, d,
                                  single_buffer_lhs=True, rider=(gated_s, xs))
            conv_p.append(cbuf)
            h_p.append(hl.reshape(bsz, d))
            conv_s.append(jnp.stack([state_conv[j, :, 1, :], state_conv[j, :, 2, :],
                                     proj_s[:dbs, d:]], axis=1))
            h_s.append(h_new[:dbs])
        else:
            w_f_t = w_fox_in_t[j, 4 * width:, :].astype(BF16)
            fox = functools.partial(_proj, up, w_fox_in_t, (j,), ncols=width, tm=pm, tn=pn,
                                    w_rows_out=True, rider=us)
            q, qs = fox(col0=0, out_dtypes=[BF16], rider_dtypes=[BF16], head_gain=q_norm_g[j])
            k, k_b, ks = fox(col0=width, out_dtypes=[F32, BF16], rider_dtypes=[F32],
                             head_gain=k_norm_g[j], single_buffer_lhs=True)
            v, v_b, vs = fox(col0=2 * width, out_dtypes=[F32, BF16], rider_dtypes=[F32],
                             single_buffer_lhs=True)
            g, gs = fox(col0=3 * width, out_dtypes=[F32], rider_dtypes=[F32])
            lf_t, c_t = _logf_prompt(up, w_f_t, b_forget[j], bsz, seq)
            c_rows = c_t.reshape(nh, bsz, seq).transpose(1, 0, 2).reshape(bsz * nh, 1, seq)
            o = _fox_attend_prompt(q, k_b, v_b, g, c_rows, bsz, seq)
            k_p.append(k.reshape(bsz, seq, nh, HEAD_DIM))
            v_p.append(v.reshape(bsz, seq, nh, HEAD_DIM))
            lf_p.append(lf_t.reshape(nh, bsz, seq).transpose(1, 2, 0))

            lfs = _logf_step(us, w_f_t.T, b_forget[j])[:dbs]
            heads = lambda a: a[:dbs].reshape(dbs, nh, HEAD_DIM)
            c_past = _page_cumsum(cache_logf_t, j, page_table)
            c_tot = c_past[:, n_pages - 1, :, PAGE_SIZE - 1]
            c_rows_s = c_past.transpose(0, 1, 3, 2).reshape(dbs, n_pages, 1, PAGE_SIZE * nh)
            lane_head = lax.broadcasted_iota(jnp.int32, (nh, PAGE_SIZE * nh), 1) % nh
            row_head = lax.broadcasted_iota(jnp.int32, (nh, PAGE_SIZE * nh), 0)
            head_bias = jnp.where(lane_head == row_head, 0.0, -jnp.inf).astype(F32)
            os_ = _fox_attend_sample(heads(qs), heads(ks), heads(vs), heads(gs),
                                     lfs.reshape(dbs, nh, 1), c_tot.reshape(dbs, nh, 1),
                                     c_rows_s, head_bias, cache_k, cache_v, j, page_table)
            os_ = jnp.pad(os_.reshape(dbs, width), ((0, srows - dbs), (0, 0))).astype(BF16)
            xp, xs = _mm_residual(o, w_fox_out, (j,), xp, 1.0, pm, pn, width,
                                  single_buffer_lhs=True, rider=(os_, xs))
            k_s.append(ks[:dbs].reshape(dbs, 1, nh, HEAD_DIM))
            v_s.append(vs[:dbs].reshape(dbs, 1, nh, HEAD_DIM))
            lf_s.append(lfs.reshape(dbs, 1, nh))
        xp, xs = _ffn_half(xp, xs, norm_g[i, 2], w_ffn_gu, w_ffn_down, (i, 1), dff)

    return (xp.reshape(bsz, seq, d), xs[:dbs].reshape(dbs, 1, d),
            jnp.stack(conv_p), jnp.stack(h_p), jnp.stack(k_p), jnp.stack(v_p), jnp.stack(lf_p),
            jnp.stack(conv_s), jnp.stack(h_s), jnp.stack(k_s), jnp.stack(v_s), jnp.stack(lf_s))
```

```python
import functools

import jax
import jax.numpy as jnp
from jax import lax
from jax.experimental import pallas as pl
from jax.experimental.pallas import tpu as pltpu

F32 = jnp.float32
BF16 = jnp.bfloat16

EPS = 1e-6
LRU_C = 8.0
LRU_BLOCK = 256
CONV_W = 4
HEAD_DIM = 128
PAGE_SIZE = 128
LOG2E = 1.4426950408889634

LANE = 128
SUBLANE = 8
BF16_ROWS = 16
VMEM_BYTES_V7X = 64 * 1024 * 1024
VMEM_LIMIT = VMEM_BYTES_V7X - 8 * 1024 * 1024

_NT_DIMS = (((1,), (1,)), ((), ()))


def _params(*semantics):
    return pltpu.CompilerParams(dimension_semantics=semantics, vmem_limit_bytes=VMEM_LIMIT)


def _weight_spec(block, lead, index_map):
    nlead = len(lead)
    return pl.BlockSpec((None,) * nlead + tuple(block),
                        lambda *g: tuple(lead) + tuple(index_map(*g)))


def _rms_body(x_ref, g_ref, o_ref):
    x = x_ref[...]
    ms = jnp.mean(x * x, axis=-1, keepdims=True)
    o_ref[...] = (x * lax.rsqrt(ms + EPS) * g_ref[...]).astype(o_ref.dtype)


def _rmsnorm(x, g):
    m, d = x.shape
    tr = min(m, 256)
    return pl.pallas_call(
        _rms_body,
        grid=(m // tr,),
        in_specs=[pl.BlockSpec((tr, d), lambda i: (i, 0)),
                  pl.BlockSpec((1, d), lambda i: (0, 0))],
        out_specs=pl.BlockSpec((tr, d), lambda i: (i, 0)),
        out_shape=jax.ShapeDtypeStruct((m, d), BF16),
        compiler_params=_params("parallel"),
        name="rmsnorm",
    )(x, g.reshape(1, d))


def _swiglu_body(a_ref, wg_ref, wu_ref, o_ref):
    a = a_ref[...]
    g = jnp.dot(a, wg_ref[...].astype(BF16), preferred_element_type=F32)
    u = jnp.dot(a, wu_ref[...].astype(BF16), preferred_element_type=F32)
    o_ref[...] = (jax.nn.silu(g) * u).astype(o_ref.dtype)


def _ffn_up(n, w_gu, lead, tm, tn):
    m, d = n.shape
    dff = w_gu.shape[-1] // 2
    nj = dff // tn
    return pl.pallas_call(
        _swiglu_body,
        grid=(m // tm, nj),
        in_specs=[pl.BlockSpec((tm, d), lambda i, j: (i, 0)),
                  _weight_spec((d, tn), lead, lambda i, j: (0, j)),
                  _weight_spec((d, tn), lead, lambda i, j: (0, j + nj))],
        out_specs=pl.BlockSpec((tm, tn), lambda i, j: (i, j)),
        out_shape=jax.ShapeDtypeStruct((m, dff), BF16),
        compiler_params=_params("parallel", "parallel"),
        name="ffn_up",
    )(n, w_gu, w_gu)


def _mm_res_body(a_ref, w_ref, r_ref, o_ref, *rest, nk, scale, emit_w):
    w = w_ref[...].astype(BF16)
    if emit_w:
        rest[0][...] = w
        rest = rest[1:]
    part = jnp.dot(a_ref[...], w, preferred_element_type=F32)
    if nk == 1:
        o_ref[...] = r_ref[...] + scale * part
        return
    acc_ref, = rest
    k = pl.program_id(2)

    @pl.when(k == 0)
    def _():
        acc_ref[...] = part

    @pl.when(jnp.logical_and(k > 0, k < nk - 1))
    def _():
        acc_ref[...] += part

    @pl.when(k == nk - 1)
    def _():
        o_ref[...] = r_ref[...] + scale * (acc_ref[...] + part)


def _mm_residual(a, w, lead, res, scale, tm, tn, tk, emit_w=False, single_buffer_lhs=False):
    m, kdim = a.shape
    n = w.shape[-1]
    nk = kdim // tk
    assert not emit_w or m == tm
    scratch = [pltpu.VMEM((tm, tn), F32)] if nk > 1 else []
    lhs_mode = pl.Buffered(1) if single_buffer_lhs else None
    out_specs = [pl.BlockSpec((tm, tn), lambda i, j, k: (i, j))]
    out_shape = [jax.ShapeDtypeStruct((m, n), F32)]
    if emit_w:
        out_specs.append(pl.BlockSpec((tk, tn), lambda i, j, k: (k, j)))
        out_shape.append(jax.ShapeDtypeStruct((kdim, n), BF16))
    outs = pl.pallas_call(
        functools.partial(_mm_res_body, nk=nk, scale=scale, emit_w=emit_w),
        grid=(m // tm, n // tn, nk),
        in_specs=[pl.BlockSpec((tm, tk), lambda i, j, k: (i, k), pipeline_mode=lhs_mode),
                  _weight_spec((tk, tn), lead, lambda i, j, k: (k, j)),
                  pl.BlockSpec((tm, tn), lambda i, j, k: (i, j))],
        out_specs=out_specs,
        out_shape=out_shape,
        scratch_shapes=scratch,
        compiler_params=_params("parallel", "parallel", "arbitrary"),
        name="mm_residual",
    )(a, w, res)
    return outs if emit_w else outs[0]


def _proj_body(a_ref, w_ref, *rest, tn, head_norm, w_rows_out):
    w = w_ref[...].astype(BF16)
    if w_rows_out:
        acc = lax.dot_general(a_ref[...], w, _NT_DIMS, preferred_element_type=F32)
    else:
        acc = jnp.dot(a_ref[...], w, preferred_element_type=F32)
    if head_norm:
        g_ref, out_refs = rest[0], rest[1:]
        g = g_ref[...]
        for hh in range(tn // HEAD_DIM):
            sl = slice(hh * HEAD_DIM, (hh + 1) * HEAD_DIM)
            blk = acc[:, sl]
            ms = jnp.mean(blk * blk, axis=-1, keepdims=True)
            y = blk * lax.rsqrt(ms + EPS) * g
            for o_ref in out_refs:
                o_ref[:, sl] = y.astype(o_ref.dtype)
    else:
        for o_ref in rest:
            o_ref[...] = acc.astype(o_ref.dtype)


def _proj(a, w, lead, col0, ncols, tm, tn, out_dtypes, head_gain=None, w_rows_out=False,
          single_buffer_lhs=False):
    m, kdim = a.shape
    joff = col0 // tn
    if w_rows_out:
        w_spec = _weight_spec((tn, kdim), lead, lambda i, j: (j + joff, 0))
    else:
        w_spec = _weight_spec((kdim, tn), lead, lambda i, j: (0, j + joff))
    lhs_mode = pl.Buffered(1) if single_buffer_lhs else None
    in_specs = [pl.BlockSpec((tm, kdim), lambda i, j: (i, 0), pipeline_mode=lhs_mode), w_spec]
    args = [a, w]
    if head_gain is not None:
        in_specs.append(pl.BlockSpec((1, HEAD_DIM), lambda i, j: (0, 0)))
        args.append(head_gain.reshape(1, HEAD_DIM))
    outs = pl.pallas_call(
        functools.partial(_proj_body, tn=tn, head_norm=head_gain is not None,
                          w_rows_out=w_rows_out),
        grid=(m // tm, ncols // tn),
        in_specs=in_specs,
        out_specs=[pl.BlockSpec((tm, tn), lambda i, j: (i, j)) for _ in out_dtypes],
        out_shape=[jax.ShapeDtypeStruct((m, ncols), dt) for dt in out_dtypes],
        compiler_params=_params("parallel", "parallel"),
        name="proj",
    )(*args)
    return outs if len(outs) > 1 else outs[0]


def _lru_gates(xc, wa_ref, wi_ref, ba, bi, lam, nblk):
    xcb = xc.astype(BF16)
    a_parts, b_parts = [], []
    for blk in range(nblk):
        sl = slice(blk * LRU_BLOCK, (blk + 1) * LRU_BLOCK)
        gate_r = jnp.dot(xcb[:, sl], wa_ref[blk], preferred_element_type=F32) + ba[:, sl]
        gate_i = jnp.dot(xcb[:, sl], wi_ref[blk], preferred_element_type=F32) + bi[:, sl]
        r = jax.nn.sigmoid(gate_r)
        ig = jax.nn.sigmoid(gate_i)
        log_a = -LRU_C * r * jax.nn.softplus(-lam[:, sl])
        a_parts.append(jnp.exp(log_a))
        th = jnp.tanh(log_a)
        b_parts.append(jnp.sqrt(-2.0 * th / (1.0 - th)) * (ig * xc[:, sl]))
    return a_parts, b_parts


def _lru_scan_body(xr_ref, y_ref, wa_ref, wi_ref, ba_ref, bi_ref, lam_ref, cw_ref, cb_ref,
                   o_ref, hl_ref, cbuf_ref, xpad, a_scr, b_scr, h_scr, *, tt, cw_cols):
    t = pl.program_id(2)
    nblk = cw_cols // LRU_BLOCK

    @pl.when(t == 0)
    def _():
        xpad[0:SUBLANE, :] = jnp.zeros((SUBLANE, cw_cols), F32)
        h_scr[...] = jnp.zeros((SUBLANE, cw_cols), F32)

    xr = xr_ref[...]
    xpad[SUBLANE:SUBLANE + tt, :] = xr
    cw = cw_ref[...]
    xc = cb_ref[...] + cw[0:1] * xpad[SUBLANE - 3:SUBLANE - 3 + tt, :]
    xc = xc + cw[1:2] * xpad[SUBLANE - 2:SUBLANE - 2 + tt, :]
    xc = xc + cw[2:3] * xpad[SUBLANE - 1:SUBLANE - 1 + tt, :]
    xc = xc + cw[3:4] * xr
    a_parts, b_parts = _lru_gates(xc, wa_ref, wi_ref, ba_ref[...], bi_ref[...], lam_ref[...], nblk)
    for blk in range(nblk):
        sl = slice(blk * LRU_BLOCK, (blk + 1) * LRU_BLOCK)
        a_scr[:, sl] = a_parts[blk]
        b_scr[:, sl] = b_parts[blk]

    row = lax.broadcasted_iota(jnp.int32, (SUBLANE, cw_cols), 0)

    def group(gi, h):
        off = pl.multiple_of(gi * SUBLANE, SUBLANE)
        a = a_scr[pl.ds(off, SUBLANE), :]
        b = b_scr[pl.ds(off, SUBLANE), :]
        for s in (1, 2, 4):
            a_sh = jnp.where(row >= s, pltpu.roll(a, s, 0), 1.0)
            b_sh = jnp.where(row >= s, pltpu.roll(b, s, 0), 0.0)
            b = a * b_sh + b
            a = a * a_sh
        hcur = a * h + b
        a_scr[pl.ds(off, SUBLANE), :] = hcur
        return jnp.broadcast_to(hcur[SUBLANE - 1:SUBLANE, :], (SUBLANE, cw_cols))

    h = lax.fori_loop(0, tt // SUBLANE, group, h_scr[...], unroll=4)
    h_scr[...] = h
    xpad[0:SUBLANE, :] = xpad[tt:tt + SUBLANE, :]
    o_ref[...] = (jax.nn.gelu(y_ref[...]) * a_scr[...]).astype(o_ref.dtype)

    @pl.when(t == pl.num_programs(2) - 1)
    def _():
        hl_ref[0] = h[0:1, :]
        cbuf_ref[0] = xr_ref[tt - (CONV_W - 1):tt, :]


def _lru_prompt(proj, bsz, seq, wa, wi, ba, bi, lam, cw, cb, tt=512, cw_cols=1024):
    d = proj.shape[1] // 2
    nc = d // cw_cols
    nt = seq // tt
    gpb = cw_cols // LRU_BLOCK
    vec = lambda: pl.BlockSpec((1, cw_cols), lambda b, c, t: (0, c))
    return pl.pallas_call(
        functools.partial(_lru_scan_body, tt=tt, cw_cols=cw_cols),
        grid=(bsz, nc, nt),
        in_specs=[pl.BlockSpec((tt, cw_cols), lambda b, c, t: (b * nt + t, nc + c)),
                  pl.BlockSpec((tt, cw_cols), lambda b, c, t: (b * nt + t, c)),
                  pl.BlockSpec((gpb, LRU_BLOCK, LRU_BLOCK), lambda b, c, t: (c, 0, 0)),
                  pl.BlockSpec((gpb, LRU_BLOCK, LRU_BLOCK), lambda b, c, t: (c, 0, 0)),
                  vec(), vec(), vec(),
                  pl.BlockSpec((CONV_W, cw_cols), lambda b, c, t: (0, c)),
                  vec()],
        out_specs=[pl.BlockSpec((tt, cw_cols), lambda b, c, t: (b * nt + t, c)),
                   pl.BlockSpec((1, 1, cw_cols), lambda b, c, t: (b, 0, c)),
                   pl.BlockSpec((1, CONV_W - 1, cw_cols), lambda b, c, t: (b, 0, c))],
        out_shape=[jax.ShapeDtypeStruct((bsz * seq, d), BF16),
                   jax.ShapeDtypeStruct((bsz, 1, d), F32),
                   jax.ShapeDtypeStruct((bsz, CONV_W - 1, d), F32)],
        scratch_shapes=[pltpu.VMEM((tt + SUBLANE, cw_cols), F32),
                        pltpu.VMEM((tt, cw_cols), F32),
                        pltpu.VMEM((tt, cw_cols), F32),
                        pltpu.VMEM((SUBLANE, cw_cols), F32)],
        compiler_params=_params("parallel", "parallel", "arbitrary"),
        name="lru_scan",
    )(proj, proj, wa, wi, ba.reshape(1, d), bi.reshape(1, d), lam.reshape(1, d), cw, cb.reshape(1, d))


def _lru_step_body(xr_ref, y_ref, b0_ref, b1_ref, b2_ref, h0_ref, wa_ref, wi_ref, ba_ref, bi_ref,
                   lam_ref, cw_ref, cb_ref, o_ref, h_ref, *, cw_cols):
    nblk = cw_cols // LRU_BLOCK
    cw = cw_ref[...]
    xr = xr_ref[...]
    xc = cb_ref[...] + cw[0:1] * b0_ref[...]
    xc = xc + cw[1:2] * b1_ref[...]
    xc = xc + cw[2:3] * b2_ref[...]
    xc = xc + cw[3:4] * xr
    a_parts, b_parts = _lru_gates(xc, wa_ref, wi_ref, ba_ref[...], bi_ref[...], lam_ref[...], nblk)
    y = y_ref[...]
    for blk in range(nblk):
        sl = slice(blk * LRU_BLOCK, (blk + 1) * LRU_BLOCK)
        h = a_parts[blk] * h0_ref[:, sl] + b_parts[blk]
        h_ref[:, sl] = h
        o_ref[:, sl] = (jax.nn.gelu(y[:, sl]) * h).astype(o_ref.dtype)


def _lru_step(proj, bufs, h0, wa, wi, ba, bi, lam, cw, cb, cw_cols=1024):
    rows = proj.shape[0]
    d = proj.shape[1] // 2
    nc = d // cw_cols
    gpb = cw_cols // LRU_BLOCK
    act = lambda off: pl.BlockSpec((rows, cw_cols), lambda c: (0, c + off))
    vec = lambda: pl.BlockSpec((1, cw_cols), lambda c: (0, c))
    return pl.pallas_call(
        functools.partial(_lru_step_body, cw_cols=cw_cols),
        grid=(nc,),
        in_specs=[act(nc), act(0), act(0), act(0), act(0), act(0),
                  pl.BlockSpec((gpb, LRU_BLOCK, LRU_BLOCK), lambda c: (c, 0, 0)),
                  pl.BlockSpec((gpb, LRU_BLOCK, LRU_BLOCK), lambda c: (c, 0, 0)),
                  vec(), vec(), vec(),
                  pl.BlockSpec((CONV_W, cw_cols), lambda c: (0, c)),
                  vec()],
        out_specs=[act(0), act(0)],
        out_shape=[jax.ShapeDtypeStruct((rows, d), BF16),
                   jax.ShapeDtypeStruct((rows, d), F32)],
        compiler_params=_params("parallel"),
        name="lru_step",
    )(proj, proj, bufs[0], bufs[1], bufs[2], h0, wa, wi,
      ba.reshape(1, d), bi.reshape(1, d), lam.reshape(1, d), cw, cb.reshape(1, d))


def _split3(x):
    hi = x.astype(BF16)
    r1 = x - hi.astype(F32)
    mid = r1.astype(BF16)
    lo = (r1 - mid.astype(F32)).astype(BF16)
    return hi, mid, lo


def _dot_x_ones(x, ones_rhs):
    hi, mid, lo = _split3(x)
    d = lambda p: jnp.dot(p, ones_rhs, preferred_element_type=F32)
    return d(hi) + d(mid) + d(lo)


def _logf_body(wft_ref, n_ref, bf_ref, u_ref, lf_ref, c_ref, carry, *, tt):
    t = pl.program_id(1)

    @pl.when(t == 0)
    def _():
        carry[...] = jnp.zeros(carry.shape, F32)

    f = lax.dot_general(wft_ref[...], n_ref[...], _NT_DIMS, preferred_element_type=F32)
    lf = jax.nn.log_sigmoid(f + bf_ref[...])
    lf_ref[...] = lf
    u = u_ref[...]
    c = carry[...]
    nh = lf.shape[0]
    for kk in range(tt // LANE):
        sl = slice(kk * LANE, (kk + 1) * LANE)
        cs = _dot_x_ones(lf[:, sl], u) + c
        c_ref[:, sl] = cs
        c = jnp.broadcast_to(cs[:, LANE - 1:LANE], (nh, LANE))
    carry[...] = c


def _logf_prompt(n, wft, b_f, bsz, seq, tt=512):
    m, d = n.shape
    nh = wft.shape[0]
    nt = seq // tt
    upper = (lax.broadcasted_iota(jnp.int32, (LANE, LANE), 0)
             <= lax.broadcasted_iota(jnp.int32, (LANE, LANE), 1)).astype(BF16)
    return pl.pallas_call(
        functools.partial(_logf_body, tt=tt),
        grid=(bsz, nt),
        in_specs=[pl.BlockSpec((nh, d), lambda b, t: (0, 0)),
                  pl.BlockSpec((tt, d), lambda b, t: (b * nt + t, 0)),
                  pl.BlockSpec((nh, 1), lambda b, t: (0, 0)),
                  pl.BlockSpec((LANE, LANE), lambda b, t: (0, 0))],
        out_specs=[pl.BlockSpec((nh, tt), lambda b, t: (0, b * nt + t)),
                   pl.BlockSpec((nh, tt), lambda b, t: (0, b * nt + t))],
        out_shape=[jax.ShapeDtypeStruct((nh, m), F32),
                   jax.ShapeDtypeStruct((nh, m), F32)],
        scratch_shapes=[pltpu.VMEM((nh, LANE), F32)],
        compiler_params=_params("parallel", "arbitrary"),
        name="logf_prompt",
    )(wft, n, b_f.reshape(nh, 1), upper)


def _flash_body(q_ref, k_ref, v_ref, g_ref, c_ref, o_ref, *, tq, nq, heads, scale):
    row = lax.broadcasted_iota(jnp.int32, (tq, tq), 0)
    col = lax.broadcasted_iota(jnp.int32, (tq, tq), 1)
    visible = col <= row

    for qi in range(nq):
        lo, hi = qi * tq, (qi + 1) * tq
        for hh in range(heads):
            hs = slice(hh * HEAD_DIM, (hh + 1) * HEAD_DIM)
            c2 = c_ref[hh] * LOG2E

            def scores(q, klo, khi):
                s = lax.dot_general(q, k_ref[klo:khi, hs], _NT_DIMS, preferred_element_type=F32)
                return s * (scale * LOG2E) - c2[:, klo:khi]

            q = q_ref[lo:hi, hs]
            s_diag = jnp.where(visible, scores(q, lo, hi), -jnp.inf)
            m = jnp.max(s_diag, axis=-1, keepdims=True)
            if qi > 0:
                s_past = scores(q, 0, lo)
                m = jnp.maximum(m, jnp.max(s_past, axis=-1, keepdims=True))
            p_diag = jnp.exp2(s_diag - m)
            l = jnp.sum(p_diag, axis=-1, keepdims=True)
            acc = jnp.dot(p_diag.astype(BF16), v_ref[lo:hi, hs], preferred_element_type=F32)
            if qi > 0:
                p_past = jnp.exp2(s_past - m)
                l = l + jnp.sum(p_past, axis=-1, keepdims=True)
                acc = acc + jnp.dot(p_past.astype(BF16), v_ref[0:lo, hs],
                                    preferred_element_type=F32)
            o_ref[lo:hi, hs] = (acc / l * jax.nn.sigmoid(g_ref[lo:hi, hs])).astype(o_ref.dtype)


def _fox_attend_prompt(q, k, v, g, c_rows, bsz, seq, tq=512, heads=4):
    m, width = q.shape
    nh = width // HEAD_DIM
    ng = nh // heads
    tile = lambda: pl.BlockSpec((seq, heads * HEAD_DIM), lambda b, h: (b, h))
    return pl.pallas_call(
        functools.partial(_flash_body, tq=tq, nq=seq // tq, heads=heads, scale=HEAD_DIM ** -0.5),
        grid=(bsz, ng),
        in_specs=[tile(), tile(), tile(), tile(),
                  pl.BlockSpec((heads, 1, seq), lambda b, h: (b * ng + h, 0, 0))],
        out_specs=tile(),
        out_shape=jax.ShapeDtypeStruct((m, width), BF16),
        compiler_params=_params("parallel", "parallel"),
        name="fox_prompt_attn",
    )(q, k, v, g, c_rows)


def _page_cumsum_body(pt_ref, *refs, pages):
    lf_refs = refs[:pages]
    u_ref, o_ref, carry = refs[pages], refs[pages + 1], refs[pages + 2]
    j = pl.program_id(1)

    @pl.when(j == 0)
    def _():
        carry[...] = jnp.zeros(carry.shape, F32)

    u = u_ref[...]
    within_page = [_dot_x_ones(lf_refs[p][0, 0], u) for p in range(pages)]
    c = carry[...]
    for p in range(pages):
        cs = within_page[p] + c
        o_ref[0, p] = cs
        c = jnp.broadcast_to(cs[:, PAGE_SIZE - 1:PAGE_SIZE], c.shape)
    carry[...] = c


def _page_cumsum(cache_logf_t, layer, page_table, pages=16):
    dbs, n_pages = page_table.shape
    nh = cache_logf_t.shape[-2]
    upper = (lax.broadcasted_iota(jnp.int32, (PAGE_SIZE, PAGE_SIZE), 0)
             <= lax.broadcasted_iota(jnp.int32, (PAGE_SIZE, PAGE_SIZE), 1)).astype(BF16)

    def page_spec(p):
        return pl.BlockSpec((1, 1, nh, PAGE_SIZE),
                            lambda b, j, pt: (layer, pt[b, j * pages + p], 0, 0))

    grid_spec = pltpu.PrefetchScalarGridSpec(
        num_scalar_prefetch=1,
        grid=(dbs, n_pages // pages),
        in_specs=[page_spec(p) for p in range(pages)]
        + [pl.BlockSpec((PAGE_SIZE, PAGE_SIZE), lambda b, j, pt: (0, 0))],
        out_specs=pl.BlockSpec((1, pages, nh, PAGE_SIZE), lambda b, j, pt: (b, j, 0, 0)),
        scratch_shapes=[pltpu.VMEM((nh, PAGE_SIZE), F32)])
    return pl.pallas_call(
        functools.partial(_page_cumsum_body, pages=pages),
        grid_spec=grid_spec,
        out_shape=jax.ShapeDtypeStruct((dbs, n_pages, nh, PAGE_SIZE), F32),
        compiler_params=_params("parallel", "arbitrary"),
        name="page_cumsum",
    )(page_table, *([cache_logf_t] * pages), upper)


def _decode_body(pt_ref, *refs, pages, scale):
    q_ref = refs[0]
    k_refs = refs[1:1 + pages]
    v_refs = refs[1 + pages:1 + 2 * pages]
    (c_ref, hb_ref, kn_ref, vn_ref, lfn_ref, ctot_ref, g_ref,
     o_ref, m_scr, l_scr, acc_scr) = refs[1 + 2 * pages:]
    j = pl.program_id(1)

    @pl.when(j == 0)
    def _():
        m_scr[...] = jnp.full(m_scr.shape, -jnp.inf, F32)
        l_scr[...] = jnp.zeros(l_scr.shape, F32)
        acc_scr[...] = jnp.zeros(acc_scr.shape, F32)

    q = q_ref[0]
    hb = hb_ref[...]
    nh = q.shape[0]
    for p in range(pages):
        k = k_refs[p][0, 0].reshape(PAGE_SIZE * nh, HEAD_DIM).astype(BF16)
        v = v_refs[p][0, 0].reshape(PAGE_SIZE * nh, HEAD_DIM).astype(BF16)
        s = lax.dot_general(q, k, _NT_DIMS, preferred_element_type=F32) * scale
        s = s - c_ref[0, p] + hb
        m_prev = m_scr[...]
        m_new = jnp.maximum(m_prev, jnp.max(s, axis=-1, keepdims=True))
        alpha = jnp.exp(m_prev - m_new)
        pr = jnp.exp(s - m_new)
        l_scr[...] = alpha * l_scr[...] + jnp.sum(pr, axis=-1, keepdims=True)
        acc_scr[...] = alpha * acc_scr[...] + jnp.dot(pr.astype(BF16), v, preferred_element_type=F32)
        m_scr[...] = m_new

    @pl.when(j == pl.num_programs(1) - 1)
    def _():
        kn = kn_ref[0].astype(BF16).astype(F32)
        s_self = jnp.sum(q.astype(F32) * kn, axis=-1, keepdims=True) * scale
        s_self = s_self - (ctot_ref[0] + lfn_ref[0])
        m_prev = m_scr[...]
        m_new = jnp.maximum(m_prev, s_self)
        alpha = jnp.exp(m_prev - m_new)
        p_self = jnp.exp(s_self - m_new)
        l = alpha * l_scr[...] + p_self
        acc = alpha * acc_scr[...] + p_self * vn_ref[0]
        o_ref[0] = (acc / l) * jax.nn.sigmoid(g_ref[0])


def _fox_attend_sample(q, k_new, v_new, g, lf_new, c_tot, c_rows, head_bias,
                       cache_k, cache_v, layer, page_table, pages=4):
    dbs, n_pages = page_table.shape
    nh = q.shape[1]
    row_w = PAGE_SIZE * nh

    def page_spec(p):
        return pl.BlockSpec((1, 1, PAGE_SIZE, nh, HEAD_DIM),
                            lambda b, j, pt: (layer, pt[b, j * pages + p], 0, 0, 0))

    tok = lambda: pl.BlockSpec((1, nh, HEAD_DIM), lambda b, j, pt: (b, 0, 0))
    col = lambda: pl.BlockSpec((1, nh, 1), lambda b, j, pt: (b, 0, 0))
    grid_spec = pltpu.PrefetchScalarGridSpec(
        num_scalar_prefetch=1,
        grid=(dbs, n_pages // pages),
        in_specs=[tok()]
        + [page_spec(p) for p in range(pages)]
        + [page_spec(p) for p in range(pages)]
        + [pl.BlockSpec((1, pages, 1, row_w), lambda b, j, pt: (b, j, 0, 0)),
           pl.BlockSpec((nh, row_w), lambda b, j, pt: (0, 0)),
           tok(), tok(), col(), col(), tok()],
        out_specs=tok(),
        scratch_shapes=[pltpu.VMEM((nh, 1), F32), pltpu.VMEM((nh, 1), F32),
                        pltpu.VMEM((nh, HEAD_DIM), F32)])
    return pl.pallas_call(
        functools.partial(_decode_body, pages=pages, scale=HEAD_DIM ** -0.5),
        grid_spec=grid_spec,
        out_shape=jax.ShapeDtypeStruct((dbs, nh, HEAD_DIM), F32),
        compiler_params=_params("parallel", "arbitrary"),
        name="fox_sample_attn",
    )(page_table, q, *([cache_k] * pages), *([cache_v] * pages),
      c_rows, head_bias, k_new, v_new, lf_new, c_tot, g)


def _logf_step_body(n_ref, wf_ref, bf_ref, o_ref):
    f = jnp.dot(n_ref[...], wf_ref[...], preferred_element_type=F32)
    o_ref[...] = jax.nn.log_sigmoid(f + bf_ref[...])


def _logf_step(n, wf, b_f):
    rows = n.shape[0]
    nh = wf.shape[1]
    return pl.pallas_call(
        _logf_step_body,
        out_shape=jax.ShapeDtypeStruct((rows, nh), F32),
        name="logf_step",
    )(n, wf, b_f.reshape(1, nh))


def _ffn_half(xp, xs, g, w_gu, w_down, lead, dff):
    srows = xs.shape[0]
    hs = _ffn_up(_rmsnorm(xs, g), w_gu, lead, srows, 512)
    xs, w_down_b = _mm_residual(hs, w_down, lead, xs, 0.5, srows, 512, dff // 2, emit_w=True)
    hp = _ffn_up(_rmsnorm(xp, g), w_gu, lead, 2048, 256)
    xp = _mm_residual(hp, w_down_b, (), xp, 0.5, 1024, 256, dff)
    return xp, xs


def kernel(x_prompt, x_sample, state_conv, state_h, cache_k, cache_v, cache_logf, page_table,
           norm_g, w_ffn_gu, w_ffn_down, w_rec_in, conv_w, conv_b, w_gate_a, b_gate_a,
           w_gate_i, b_gate_i, lru_lambda, w_rec_out, w_fox_in, b_forget, q_norm_g, k_norm_g,
           w_fox_out):
    bsz, seq, d = x_prompt.shape
    dbs, dseq, _ = x_sample.shape
    assert dseq == 1, "sample group decodes one token per sequence"
    depth = norm_g.shape[0]
    nh = b_forget.shape[1]
    width = nh * HEAD_DIM
    dff = w_ffn_down.shape[2]
    n_pages = page_table.shape[1]
    srows = BF16_ROWS
    assert dbs <= srows

    w_ga_b = w_gate_a.astype(BF16)
    w_gi_b = w_gate_i.astype(BF16)

    xp = x_prompt.reshape(bsz * seq, d)
    xs = jnp.pad(x_sample.reshape(dbs, d), ((0, srows - dbs), (0, 0)))

    pm, pn = 2048, 256
    sn = 512
    w_fox_in_t = w_fox_in.transpose(0, 2, 1)
    cache_logf_t = cache_logf.transpose(0, 1, 3, 2)

    conv_p, h_p, k_p, v_p, lf_p = [], [], [], [], []
    conv_s, h_s, k_s, v_s, lf_s = [], [], [], [], []
    for i in range(depth):
        j = i // 2
        xp, xs = _ffn_half(xp, xs, norm_g[i, 0], w_ffn_gu, w_ffn_down, (i, 0), dff)
        up = _rmsnorm(xp, norm_g[i, 1])
        us = _rmsnorm(xs, norm_g[i, 1])
        if i % 2 == 0:
            lru_w = (w_ga_b[j], w_gi_b[j], b_gate_a[j], b_gate_i[j], lru_lambda[j], conv_w[j], conv_b[j])
            proj_p = _proj(up, w_rec_in, (j,), 0, 2 * d, pm, pn, [F32])
            gated_p, hl, cbuf = _lru_prompt(proj_p, bsz, seq, *lru_w)
            xp = _mm_residual(gated_p, w_rec_out, (j,), xp, 1.0, pm, pn, d, single_buffer_lhs=True)
            conv_p.append(cbuf)
            h_p.append(hl.reshape(bsz, d))

            proj_s = _proj(us, w_rec_in, (j,), 0, 2 * d, srows, sn, [F32])
            pad_rows = lambda a: jnp.pad(a, ((0, srows - dbs), (0, 0)))
            bufs = [pad_rows(state_conv[j, :, r, :]) for r in range(CONV_W - 1)]
            gated_s, h_new = _lru_step(proj_s, bufs, pad_rows(state_h[j]), *lru_w)
            xs = _mm_residual(gated_s, w_rec_out, (j,), xs, 1.0, srows, sn, d)
            conv_s.append(jnp.stack([state_conv[j, :, 1, :], state_conv[j, :, 2, :],
                                     proj_s[:dbs, d:]], axis=1))
            h_s.append(h_new[:dbs])
        else:
            w_f_t = w_fox_in_t[j, 4 * width:, :].astype(BF16)
            fox = functools.partial(_proj, w=w_fox_in_t, lead=(j,), ncols=width, w_rows_out=True)
            q = fox(up, col0=0, tm=pm, tn=pn, out_dtypes=[BF16], head_gain=q_norm_g[j])
            k, k_b = fox(up, col0=width, tm=pm, tn=pn, out_dtypes=[F32, BF16], head_gain=k_norm_g[j],
                         single_buffer_lhs=True)
            v, v_b = fox(up, col0=2 * width, tm=pm, tn=pn, out_dtypes=[F32, BF16],
                         single_buffer_lhs=True)
            g = fox(up, col0=3 * width, tm=pm, tn=pn, out_dtypes=[F32])
            lf_t, c_t = _logf_prompt(up, w_f_t, b_forget[j], bsz, seq)
            c_rows = c_t.reshape(nh, bsz, seq).transpose(1, 0, 2).reshape(bsz * nh, 1, seq)
            o = _fox_attend_prompt(q, k_b, v_b, g, c_rows, bsz, seq)
            xp = _mm_residual(o, w_fox_out, (j,), xp, 1.0, pm, pn, width, single_buffer_lhs=True)
            k_p.append(k.reshape(bsz, seq, nh, HEAD_DIM))
            v_p.append(v.reshape(bsz, seq, nh, HEAD_DIM))
            lf_p.append(lf_t.reshape(nh, bsz, seq).transpose(1, 2, 0))

            qs = fox(us, col0=0, tm=srows, tn=sn, out_dtypes=[BF16], head_gain=q_norm_g[j])
            ks = fox(us, col0=width, tm=srows, tn=sn, out_dtypes=[F32], head_gain=k_norm_g[j])
            vs = fox(us, col0=2 * width, tm=srows, tn=sn, out_dtypes=[F32])
            gs = fox(us, col0=3 * width, tm=srows, tn=sn, out_dtypes=[F32])
            lfs = _logf_step(us, w_f_t.T, b_forget[j])[:dbs]
            heads = lambda a: a[:dbs].reshape(dbs, nh, HEAD_DIM)
            c_past = _page_cumsum(cache_logf_t, j, page_table)
            c_tot = c_past[:, n_pages - 1, :, PAGE_SIZE - 1]
            c_rows_s = c_past.transpose(0, 1, 3, 2).reshape(dbs, n_pages, 1, PAGE_SIZE * nh)
            lane_head = lax.broadcasted_iota(jnp.int32, (nh, PAGE_SIZE * nh), 1) % nh
            row_head = lax.broadcasted_iota(jnp.int32, (nh, PAGE_SIZE * nh), 0)
            head_bias = jnp.where(lane_head == row_head, 0.0, -jnp.inf).astype(F32)
            os_ = _fox_attend_sample(heads(qs), heads(ks), heads(vs), heads(gs),
                                     lfs.reshape(dbs, nh, 1), c_tot.reshape(dbs, nh, 1),
                                     c_rows_s, head_bias, cache_k, cache_v, j, page_table)
            os_ = jnp.pad(os_.reshape(dbs, width), ((0, srows - dbs), (0, 0))).astype(BF16)
            xs = _mm_residual(os_, w_fox_out, (j,), xs, 1.0, srows, sn, width)
            k_s.append(ks[:dbs].reshape(dbs, 1, nh, HEAD_DIM))
            v_s.append(vs[:dbs].reshape(dbs, 1, nh, HEAD_DIM))
            lf_s.append(lfs.reshape(dbs, 1, nh))
        xp, xs = _ffn_half(xp, xs, norm_g[i, 2], w_ffn_gu, w_ffn_down, (i, 1), dff)

    return (xp.reshape(bsz, seq, d), xs[:dbs].reshape(dbs, 1, d),
            jnp.stack(conv_p), jnp.stack(h_p), jnp.stack(k_p), jnp.stack(v_p), jnp.stack(lf_p),
            jnp.stack(conv_s), jnp.stack(h_s), jnp.stack(k_s), jnp.stack(v_s), jnp.stack(lf_s))
```
